```python
import jax, jax.numpy as jnp
from jax import lax
import numpy as np

D_MODEL = 1024
BATCH = 4
SEQ = 4096
DEPTH = 2

N_MIXERS = 2
NORM_EPS = 1e-6
D_RNN = 128 * round(4 * D_MODEL / (3 * 128))
LRU_BLOCKS = 16
LRU_BLOCK = D_RNN // LRU_BLOCKS
LRU_C = 8.0
CONV_WIDTH = 4
FOX_HEADS = 16
FOX_HEAD_DIM = D_MODEL // FOX_HEADS
QUERY_BLOCK = 128
PEER_HEADS = 8
PEER_KEYS = 128
PEER_EXPERTS = PEER_KEYS * PEER_KEYS
PEER_TOPK = 16
PEER_QDIM = 256
PEER_HALF = PEER_QDIM // 2
TOKEN_CHUNK = 128

kernel_name = 'hybrid_rglru_fox_peer_adaln'


def rms_norm(x, g):
    xf = x.astype(jnp.float32)
    y = xf * lax.rsqrt(jnp.mean(xf * xf, axis=-1, keepdims=True) + NORM_EPS)
    return (y * g.astype(jnp.float32)).astype(x.dtype)


def ada_modulation(c, w, b):
    mod = jax.nn.silu(c) @ w + b
    shift, scale, gate = jnp.split(mod, 3, axis=-1)
    return shift[:, None, :], scale[:, None, :], gate[:, None, :]


def causal_depthwise_conv(x, w, b):
    y = lax.conv_general_dilated(x, w[:, None, :], window_strides=(1,), padding=[(CONV_WIDTH - 1, 0)], dimension_numbers=('NWC', 'WIO', 'NWC'), feature_group_count=x.shape[-1])
    return y + b


def _linear_recurrence_combine(left, right):
    a_l, b_l = left
    a_r, b_r = right
    return a_l * a_r, a_r * b_l + b_r


def rglru_mixer(h, in_w, conv_w, conv_b, ra_w, ra_b, ri_w, ri_b, lam, out_w):
    B, S, _ = h.shape
    xb, gb = jnp.split(h @ in_w, 2, axis=-1)
    xb = causal_depthwise_conv(xb, conv_w, conv_b)
    xblk = xb.reshape(B, S, LRU_BLOCKS, LRU_BLOCK)
    r = jax.nn.sigmoid(jnp.einsum('bsnc,ncd->bsnd', xblk, ra_w).reshape(B, S, D_RNN) + ra_b)
    i = jax.nn.sigmoid(jnp.einsum('bsnc,ncd->bsnd', xblk, ri_w).reshape(B, S, D_RNN) + ri_b)
    log_a = -LRU_C * jax.nn.softplus(-lam.astype(jnp.float32)) * r.astype(jnp.float32)
    a = jnp.exp(log_a)
    u = jnp.sqrt(-jnp.expm1(2.0 * log_a)) * (i * xb).astype(jnp.float32)
    _, hs = lax.associative_scan(_linear_recurrence_combine, (a, u), axis=1)
    y = hs.astype(h.dtype) * jax.nn.gelu(gb, approximate=False)
    return y @ out_w


def fox_mixer(h, in_w, f_b, q_norm_g, k_norm_g, out_w):
    B, S, D = h.shape
    H, E = FOX_HEADS, FOX_HEAD_DIM
    n_blk = S // QUERY_BLOCK
    q, k, v, f_logit, og = jnp.split(h @ in_w, [D, 2 * D, 3 * D, 3 * D + H], axis=-1)
    q = rms_norm(q.reshape(B, S, H, E), q_norm_g)
    k = rms_norm(k.reshape(B, S, H, E), k_norm_g)
    v = v.reshape(B, S, H, E)
    log_f = jax.nn.log_sigmoid(f_logit.astype(jnp.float32) + f_b.astype(jnp.float32))
    cum = jnp.cumsum(log_f, axis=1).transpose(0, 2, 1)
    q_blocks = q.reshape(B, n_blk, QUERY_BLOCK, H, E).transpose(1, 0, 3, 2, 4)
    cum_blocks = cum.reshape(B, H, n_blk, QUERY_BLOCK).transpose(2, 0, 1, 3)
    pos_blocks = jnp.arange(S, dtype=jnp.int32).reshape(n_blk, QUERY_BLOCK)
    key_pos = jnp.arange(S, dtype=jnp.int32)
    scale = E ** -0.5

    def attend_block(args):
        qb, cq, qpos = args
        s = jnp.einsum('bhqe,bkhe->bhqk', qb, k).astype(jnp.float32) * scale
        s = s + cq[..., None] - cum[:, :, None, :]
        s = jnp.where(qpos[:, None] >= key_pos[None, :], s, -jnp.inf)
        p = jax.nn.softmax(s, axis=-1)
        return jnp.einsum('bhqk,bkhe->bqhe', p.astype(v.dtype), v)

    o = lax.map(attend_block, (q_blocks, cum_blocks, pos_blocks))
    o = o.transpose(1, 0, 2, 3, 4).reshape(B, S, H, E)
    o = o * jax.nn.sigmoid(og).reshape(B, S, H, E)
    return o.reshape(B, S, D) @ out_w


def peer_ffn(h, q_w, subkey1, subkey2, expert_u, expert_v):
    B, S, D = h.shape
    chunks = h.reshape(B * S // TOKEN_CHUNK, TOKEN_CHUNK, D)

    def retrieve_chunk(xc):
        q = (xc @ q_w).reshape(TOKEN_CHUNK, PEER_HEADS, 2, PEER_HALF)
        s1 = jnp.einsum('thd,hnd->thn', q[:, :, 0], subkey1).astype(jnp.float32)
        s2 = jnp.einsum('thd,hnd->thn', q[:, :, 1], subkey2).astype(jnp.float32)
        v1, i1 = lax.top_k(s1, PEER_TOPK)
        v2, i2 = lax.top_k(s2, PEER_TOPK)
        cand = (v1[..., :, None] + v2[..., None, :]).reshape(TOKEN_CHUNK, PEER_HEADS, PEER_TOPK * PEER_TOPK)
        top_s, top_c = lax.top_k(cand, PEER_TOPK)
        e1 = jnp.take_along_axis(i1, top_c // PEER_TOPK, axis=-1)
        e2 = jnp.take_along_axis(i2, top_c % PEER_TOPK, axis=-1)
        expert = e1 * PEER_KEYS + e2
        g = jax.nn.softmax(top_s, axis=-1)
        u = jnp.take(expert_u, expert, axis=0)
        act = jax.nn.gelu(jnp.einsum('thkd,td->thk', u, xc).astype(jnp.float32), approximate=False)
        w = (g * act).astype(xc.dtype)
        return jnp.einsum('thk,thkd->td', w, jnp.take(expert_v, expert, axis=0))

    return lax.map(retrieve_chunk, chunks).reshape(B, S, D)


def _mod_params(nrm):
    return nrm((D_MODEL, 3 * D_MODEL), 0.5 * D_MODEL ** -0.5), nrm((3 * D_MODEL,), 0.02)


def setup_inputs(seed: int = 0) -> dict:
    key = jax.random.key(seed)
    ks = iter(jax.random.split(key, 64))

    def nrm(shape, scale):
        return jax.random.normal(next(ks), shape, jnp.float32) * scale

    def gain(n):
        return 1.0 + nrm((n,), 0.05)

    def peer_params():
        return (nrm((D_MODEL, PEER_HEADS * PEER_QDIM), D_MODEL ** -0.5),
                nrm((PEER_HEADS, PEER_KEYS, PEER_HALF), PEER_HALF ** -0.5),
                nrm((PEER_HEADS, PEER_KEYS, PEER_HALF), PEER_HALF ** -0.5),
                nrm((PEER_EXPERTS, D_MODEL), D_MODEL ** -0.5),
                nrm((PEER_EXPERTS, D_MODEL), PEER_HEADS ** -0.5))

    x = nrm((BATCH, SEQ, D_MODEL), 1.0)
    c = nrm((BATCH, D_MODEL), 1.0)
    l0_mix_norm_g = gain(D_MODEL)
    l0_mix_mod_w, l0_mix_mod_b = _mod_params(nrm)
    l0_lru_in_w = nrm((D_MODEL, 2 * D_RNN), D_MODEL ** -0.5)
    l0_lru_conv_w = nrm((CONV_WIDTH, D_RNN), CONV_WIDTH ** -0.5)
    l0_lru_conv_b = nrm((D_RNN,), 0.02)
    l0_lru_ra_w = nrm((LRU_BLOCKS, LRU_BLOCK, LRU_BLOCK), LRU_BLOCK ** -0.5)
    l0_lru_ra_b = nrm((D_RNN,), 0.02)
    l0_lru_ri_w = nrm((LRU_BLOCKS, LRU_BLOCK, LRU_BLOCK), LRU_BLOCK ** -0.5)
    l0_lru_ri_b = nrm((D_RNN,), 0.02)
    a_pow_c = jax.random.uniform(next(ks), (D_RNN,), jnp.float32, minval=0.9, maxval=0.999)
    a0 = a_pow_c ** (1.0 / LRU_C)
    l0_lru_lambda = jnp.log(a0) - jnp.log1p(-a0)
    l0_lru_out_w = nrm((D_RNN, D_MODEL), D_RNN ** -0.5)
    l0_ffn_norm_g = gain(D_MODEL)
    l0_ffn_mod_w, l0_ffn_mod_b = _mod_params(nrm)
    l0_peer_q_w, l0_peer_subkey1, l0_peer_subkey2, l0_peer_u, l0_peer_v = peer_params()
    l1_mix_norm_g = gain(D_MODEL)
    l1_mix_mod_w, l1_mix_mod_b = _mod_params(nrm)
    l1_fox_in_w = nrm((D_MODEL, 4 * D_MODEL + FOX_HEADS), D_MODEL ** -0.5)
    l1_fox_f_b = jax.random.uniform(next(ks), (FOX_HEADS,), jnp.float32, minval=1.0, maxval=6.0)
    l1_fox_q_norm_g = gain(FOX_HEAD_DIM)
    l1_fox_k_norm_g = gain(FOX_HEAD_DIM)
    l1_fox_out_w = nrm((D_MODEL, D_MODEL), D_MODEL ** -0.5)
    l1_ffn_norm_g = gain(D_MODEL)
    l1_ffn_mod_w, l1_ffn_mod_b = _mod_params(nrm)
    l1_peer_q_w, l1_peer_subkey1, l1_peer_subkey2, l1_peer_u, l1_peer_v = peer_params()
    return {'x': x, 'c': c,
            'l0_mix_norm_g': l0_mix_norm_g, 'l0_mix_mod_w': l0_mix_mod_w, 'l0_mix_mod_b': l0_mix_mod_b,
            'l0_lru_in_w': l0_lru_in_w, 'l0_lru_conv_w': l0_lru_conv_w, 'l0_lru_conv_b': l0_lru_conv_b,
            'l0_lru_ra_w': l0_lru_ra_w, 'l0_lru_ra_b': l0_lru_ra_b, 'l0_lru_ri_w': l0_lru_ri_w, 'l0_lru_ri_b': l0_lru_ri_b,
            'l0_lru_lambda': l0_lru_lambda, 'l0_lru_out_w': l0_lru_out_w,
            'l0_ffn_norm_g': l0_ffn_norm_g, 'l0_ffn_mod_w': l0_ffn_mod_w, 'l0_ffn_mod_b': l0_ffn_mod_b,
            'l0_peer_q_w': l0_peer_q_w, 'l0_peer_subkey1': l0_peer_subkey1, 'l0_peer_subkey2': l0_peer_subkey2,
            'l0_peer_u': l0_peer_u, 'l0_peer_v': l0_peer_v,
            'l1_mix_norm_g': l1_mix_norm_g, 'l1_mix_mod_w': l1_mix_mod_w, 'l1_mix_mod_b': l1_mix_mod_b,
            'l1_fox_in_w': l1_fox_in_w, 'l1_fox_f_b': l1_fox_f_b, 'l1_fox_q_norm_g': l1_fox_q_norm_g,
            'l1_fox_k_norm_g': l1_fox_k_norm_g, 'l1_fox_out_w': l1_fox_out_w,
            'l1_ffn_norm_g': l1_ffn_norm_g, 'l1_ffn_mod_w': l1_ffn_mod_w, 'l1_ffn_mod_b': l1_ffn_mod_b,
            'l1_peer_q_w': l1_peer_q_w, 'l1_peer_subkey1': l1_peer_subkey1, 'l1_peer_subkey2': l1_peer_subkey2,
            'l1_peer_u': l1_peer_u, 'l1_peer_v': l1_peer_v}


def reference(x, c,
              l0_mix_norm_g, l0_mix_mod_w, l0_mix_mod_b,
              l0_lru_in_w, l0_lru_conv_w, l0_lru_conv_b,
              l0_lru_ra_w, l0_lru_ra_b, l0_lru_ri_w, l0_lru_ri_b,
              l0_lru_lambda, l0_lru_out_w,
              l0_ffn_norm_g, l0_ffn_mod_w, l0_ffn_mod_b,
              l0_peer_q_w, l0_peer_subkey1, l0_peer_subkey2, l0_peer_u, l0_peer_v,
              l1_mix_norm_g, l1_mix_mod_w, l1_mix_mod_b,
              l1_fox_in_w, l1_fox_f_b, l1_fox_q_norm_g, l1_fox_k_norm_g, l1_fox_out_w,
              l1_ffn_norm_g, l1_ffn_mod_w, l1_ffn_mod_b,
              l1_peer_q_w, l1_peer_subkey1, l1_peer_subkey2, l1_peer_u, l1_peer_v):
    mix_norms = [(l0_mix_norm_g, l0_mix_mod_w, l0_mix_mod_b), (l1_mix_norm_g, l1_mix_mod_w, l1_mix_mod_b)]
    mixer_params = [(l0_lru_in_w, l0_lru_conv_w, l0_lru_conv_b, l0_lru_ra_w, l0_lru_ra_b, l0_lru_ri_w, l0_lru_ri_b, l0_lru_lambda, l0_lru_out_w),
                    (l1_fox_in_w, l1_fox_f_b, l1_fox_q_norm_g, l1_fox_k_norm_g, l1_fox_out_w)]
    ffn_norms = [(l0_ffn_norm_g, l0_ffn_mod_w, l0_ffn_mod_b), (l1_ffn_norm_g, l1_ffn_mod_w, l1_ffn_mod_b)]
    peer_params = [(l0_peer_q_w, l0_peer_subkey1, l0_peer_subkey2, l0_peer_u, l0_peer_v),
                   (l1_peer_q_w, l1_peer_subkey1, l1_peer_subkey2, l1_peer_u, l1_peer_v)]
    for layer in range(DEPTH):
        g, mw, mb = mix_norms[layer]
        shift, scale, gate = ada_modulation(c, mw, mb)
        h = rms_norm(x, g) * (1.0 + scale) + shift
        mixer = rglru_mixer if layer % N_MIXERS == 0 else fox_mixer
        x = x + gate * mixer(h, *mixer_params[layer])
        g, mw, mb = ffn_norms[layer]
        shift, scale, gate = ada_modulation(c, mw, mb)
        h = rms_norm(x, g) * (1.0 + scale) + shift
        x = x + gate * peer_ffn(h, *peer_params[layer])
    return x
```

```python
import functools

import jax
import jax.numpy as jnp
from jax import lax
from jax.experimental import pallas as pl
from jax.experimental.pallas import tpu as pltpu

F32 = jnp.float32
BF16 = jnp.bfloat16

NORM_EPS = 1e-6
LRU_BLOCKS = 16
LRU_C = 8.0
CONV_WIDTH = 4
FOX_HEADS = 16
FOX_HEAD_DIM = 64
PEER_HEADS = 8
PEER_KEYS = 128
PEER_TOPK = 16
PEER_HALF = 128

LANES = 128
SUBLANES = 8
VMEM_LIMIT = 56 * 1024 * 1024

_SQRT_HALF = 0.7071067811865476
_NEG_INF = float("-inf")


def _gelu(x):
    return 0.5 * x * (1.0 + lax.erf(x * _SQRT_HALF))


def _softplus(z):
    return jnp.maximum(z, 0.0) + jnp.log1p(jnp.exp(-jnp.abs(z)))


def _norm_mod(x, g, shift, scale):
    ms = jnp.mean(x * x, axis=-1, keepdims=True)
    y = x * lax.rsqrt(ms + NORM_EPS) * g
    return y * (1.0 + scale) + shift


def _cparams(sem):
    return pltpu.CompilerParams(dimension_semantics=sem, vmem_limit_bytes=VMEM_LIMIT)


def _mod_kernel(c_ref, w_ref, b_ref, o_ref):
    c = c_ref[...]
    s = c * jax.nn.sigmoid(c)
    o_ref[...] = jnp.dot(s.astype(BF16), w_ref[...].astype(BF16), preferred_element_type=F32) + b_ref[...]


def _ada_mod(c_pad, w, b, n_batch):
    d, d3 = w.shape
    tn = 512
    mod = pl.pallas_call(
        _mod_kernel,
        grid=(d3 // tn,),
        in_specs=[pl.BlockSpec(c_pad.shape, lambda j: (0, 0)),
                  pl.BlockSpec((d, tn), lambda j: (0, j)),
                  pl.BlockSpec((1, tn), lambda j: (0, j))],
        out_specs=pl.BlockSpec((c_pad.shape[0], tn), lambda j: (0, j)),
        out_shape=jax.ShapeDtypeStruct((c_pad.shape[0], d3), F32),
        compiler_params=_cparams(("arbitrary",)),
        name="ada_mod",
    )(c_pad, w, b.reshape(1, d3))
    mod = mod[:n_batch]
    shift, scale, gate = jnp.split(mod, 3, axis=-1)
    return shift[:, None, :], scale[:, None, :], gate[:, None, :]


def _lru_kernel(x_ref, g_ref, sh_ref, sc_ref, gt_ref, inw_ref, cw_ref, cb_ref, gw_ref, gb_ref, lam_ref, ow_ref,
                o_ref, y3_ref, xc3_ref, xcb_ref, gz3_ref, yob_ref, xprev_ref, hprev_ref, *, tb, nct):
    s = pl.program_id(1)

    @pl.when(s == 0)
    def _():
        xprev_ref[...] = jnp.zeros_like(xprev_ref)
        hprev_ref[...] = jnp.zeros_like(hprev_ref)

    x = x_ref[...]
    h = _norm_mod(x, g_ref[...], sh_ref[0], sc_ref[0])
    y = jnp.dot(h.astype(BF16), inw_ref[...], preferred_element_type=F32)
    for c in range(2 * nct):
        y3_ref[c] = y[:, c * LANES:(c + 1) * LANES]

    for c in range(nct):
        xb = y3_ref[c]
        full = jnp.concatenate([xprev_ref[c], xb], axis=0)
        xprev_ref[c] = xb[tb - SUBLANES:, :]
        cw = cw_ref[c]
        xc = cb_ref[c] + cw[CONV_WIDTH - 1:CONV_WIDTH] * xb
        for j in range(1, CONV_WIDTH):
            shifted = pltpu.roll(full, j, axis=0)[SUBLANES:, :]
            xc = xc + cw[CONV_WIDTH - 1 - j:CONV_WIDTH - j] * shifted
        xc3_ref[c] = xc
        xcb_ref[:, c * LANES:(c + 1) * LANES] = xc.astype(BF16)

    gz = jnp.dot(xcb_ref[...], gw_ref[...], preferred_element_type=F32)
    for c in range(2 * nct):
        gz3_ref[c] = gz[:, c * LANES:(c + 1) * LANES]

    row = lax.broadcasted_iota(jnp.int32, (tb, LANES), 0)

    def tile_body(c, carry):
        xc = xc3_ref[c]
        r = jax.nn.sigmoid(gz3_ref[c] + gb_ref[c])
        i = jax.nn.sigmoid(gz3_ref[nct + c] + gb_ref[nct + c])
        log_a = (-LRU_C * _softplus(-lam_ref[c])) * r
        a = jnp.exp(log_a)
        u = jnp.sqrt(1.0 - jnp.exp(2.0 * log_a)) * (i * xc)
        k = 1
        while k < tb:
            keep = row >= k
            a_sh = jnp.where(keep, pltpu.roll(a, k, axis=0), 1.0)
            u_sh = jnp.where(keep, pltpu.roll(u, k, axis=0), 0.0)
            u = a * u_sh + u
            a = a * a_sh
            k *= 2
        hs = u + a * hprev_ref[c]
        hprev_ref[c] = hs[tb - 1:tb, :]
        y3_ref[c] = hs * _gelu(y3_ref[nct + c])
        return carry

    lax.fori_loop(0, nct, tile_body, 0)

    for c in range(nct):
        yob_ref[:, c * LANES:(c + 1) * LANES] = y3_ref[c].astype(BF16)
    out = jnp.dot(yob_ref[...], ow_ref[...], preferred_element_type=F32)
    o_ref[...] = x + gt_ref[0] * out


def _lru_layer(x2, n_batch, seq, g, shift, scale, gate, in_w, conv_w, conv_b, ra_w, ra_b, ri_w, ri_b, lam, out_w):
    n, d = x2.shape
    c_rnn = conv_w.shape[1]
    nct = c_rnn // LANES
    tb = 256
    nsb = seq // tb
    eye = jnp.eye(LRU_BLOCKS, dtype=F32)
    ra_d = jnp.einsum('nm,ncd->ncmd', eye, ra_w).reshape(c_rnn, c_rnn)
    ri_d = jnp.einsum('nm,ncd->ncmd', eye, ri_w).reshape(c_rnn, c_rnn)
    gw = jnp.concatenate([ra_d, ri_d], axis=1).astype(BF16)
    gb = jnp.concatenate([ra_b, ri_b]).reshape(2 * nct, 1, LANES)
    cw = conv_w.reshape(CONV_WIDTH, nct, LANES).transpose(1, 0, 2)
    cb = conv_b.reshape(nct, 1, LANES)
    lam3 = lam.reshape(nct, 1, LANES)
    const2 = lambda b, s: (0, 0)
    const3 = lambda b, s: (0, 0, 0)
    vec = pl.BlockSpec((1, 1, d), lambda b, s: (b, 0, 0))
    return pl.pallas_call(
        functools.partial(_lru_kernel, tb=tb, nct=nct),
        grid=(n_batch, nsb),
        in_specs=[pl.BlockSpec((tb, d), lambda b, s: (b * nsb + s, 0)),
                  pl.BlockSpec((1, d), const2), vec, vec, vec,
                  pl.BlockSpec((d, 2 * c_rnn), const2),
                  pl.BlockSpec((nct, CONV_WIDTH, LANES), const3),
                  pl.BlockSpec((nct, 1, LANES), const3),
                  pl.BlockSpec((c_rnn, 2 * c_rnn), const2),
                  pl.BlockSpec((2 * nct, 1, LANES), const3),
                  pl.BlockSpec((nct, 1, LANES), const3),
                  pl.BlockSpec((c_rnn, d), const2)],
        out_specs=pl.BlockSpec((tb, d), lambda b, s: (b * nsb + s, 0)),
        out_shape=jax.ShapeDtypeStruct((n, d), F32),
        scratch_shapes=[pltpu.VMEM((2 * nct, tb, LANES), F32),
                        pltpu.VMEM((nct, tb, LANES), F32),
                        pltpu.VMEM((tb, c_rnn), BF16),
                        pltpu.VMEM((2 * nct, tb, LANES), F32),
                        pltpu.VMEM((tb, c_rnn), BF16),
                        pltpu.VMEM((nct, SUBLANES, LANES), F32),
                        pltpu.VMEM((nct, 1, LANES), F32)],
        compiler_params=_cparams(("arbitrary", "arbitrary")),
        name="rglru_mixer",
    )(x2, g.reshape(1, d), shift, scale, gate, in_w.astype(BF16), cw, cb, gw, gb, lam3, out_w.astype(BF16))


_UNRANKED = 99.0


def _top16_rows(cur, dst_ref):
    rank = jnp.full(cur.shape, _UNRANKED, F32)
    m = None
    for r in range(PEER_TOPK):
        m = jnp.max(cur, axis=0, keepdims=True)
        if dst_ref is not None:
            dst_ref[r:r + 1, :] = m
        hit = cur == m
        rank = jnp.where(hit, float(r), rank)
        if r + 1 < PEER_TOPK:
            cur = jnp.where(hit, _NEG_INF, cur)
    return rank, m


def _peer_kernel(x_ref, g_ref, sh_ref, sc_ref, gt_ref, qwt_ref, k1_ref, k2_ref, u_ref, vt_ref, o_ref,
                 ht_ref, qt_ref, s2_ref, p2_ref, th_ref, p1_ref, v1_ref, v2_ref, at_ref, wt_ref, acc_ref,
                 *, tm, me):
    e = pl.program_id(1)
    n_e = pl.num_programs(1)
    nchunk = tm // LANES
    nslab = me // PEER_KEYS

    @pl.when(e == 0)
    def _prologue():
        h = _norm_mod(x_ref[...], g_ref[...], sh_ref[0], sc_ref[0])
        ht = h.T.astype(BF16)
        ht_ref[...] = ht
        qt = jnp.dot(qwt_ref[...], ht, preferred_element_type=F32)
        for i in range(2 * PEER_HEADS):
            qt_ref[i] = qt[i * PEER_HALF:(i + 1) * PEER_HALF, :].astype(BF16)
        acc_ref[...] = jnp.zeros_like(acc_ref)

        def head_body(hh, carry):
            s1 = jnp.dot(k1_ref[hh], qt_ref[2 * hh], preferred_element_type=F32)
            s2 = jnp.dot(k2_ref[hh], qt_ref[2 * hh + 1], preferred_element_type=F32)
            rank1, _ = _top16_rows(s1, v1_ref)
            rank2, _ = _top16_rows(s2, v2_ref)
            v2 = v2_ref[...]
            sums = [v1_ref[i:i + 1, :] + v2 for i in range(PEER_TOPK)]
            cand = jnp.concatenate(sums, axis=0)
            _, tau = _top16_rows(cand, None)
            m1 = v1_ref[0:1, :]
            m2 = v2_ref[0:1, :]
            z = jnp.sum(jnp.where(cand >= tau, jnp.exp(cand - (m1 + m2)), 0.0), axis=0, keepdims=True)
            cnt = jnp.zeros(s1.shape, F32)
            for i in range(PEER_TOPK):
                cnt_i = jnp.sum(jnp.where(sums[i] >= tau, 1.0, 0.0), axis=0, keepdims=True)
                cnt = jnp.where(rank1 == float(i), cnt_i, cnt)
            th_ref[hh] = cnt
            p1_ref[hh] = jnp.exp(s1 - m1) * (1.0 / z)
            s2_ref[hh] = rank2
            p2_ref[hh] = jnp.exp(s2 - m2)
            return carry

        lax.fori_loop(0, PEER_HEADS, head_body, 0)

    at_ref[...] = jnp.dot(u_ref[...], ht_ref[...], preferred_element_type=F32)

    half = PEER_KEYS // 2

    e1_0 = pl.multiple_of(e * nslab, SUBLANES)

    def half_body(hf, carry):
        k0 = pl.multiple_of(hf * half, half)
        for c in range(nchunk):
            lanes = slice(c * LANES, (c + 1) * LANES)
            cnts = [th_ref[hh, pl.ds(e1_0, nslab), lanes] for hh in range(PEER_HEADS)]
            p1s = [p1_ref[hh, pl.ds(e1_0, nslab), lanes] for hh in range(PEER_HEADS)]
            for sl in range(nslab):
                rows = pl.ds(sl * PEER_KEYS + k0, half)
                a = at_ref[rows, lanes]
                gsum = jnp.zeros((half, LANES), F32)
                for hh in range(PEER_HEADS):
                    rank2 = s2_ref[hh, pl.ds(k0, half), lanes]
                    p2 = p2_ref[hh, pl.ds(k0, half), lanes]
                    gsum = gsum + jnp.where(rank2 < cnts[hh][sl:sl + 1, :], p2, 0.0) * p1s[hh][sl:sl + 1, :]
                wt_ref[rows, lanes] = (gsum * _gelu(a)).astype(BF16)
        return carry

    lax.fori_loop(0, 2, half_body, 0)

    acc_ref[...] += jnp.dot(vt_ref[...], wt_ref[...], preferred_element_type=F32)

    @pl.when(e == n_e - 1)
    def _fin():
        o_ref[...] = x_ref[...] + gt_ref[0] * acc_ref[...].T


def _peer_layer(x2, n_batch, seq, g, shift, scale, gate, q_w, subkey1, subkey2, expert_u, expert_v):
    n, d = x2.shape
    n_exp = expert_u.shape[0]
    tm = min(512, seq)
    me = SUBLANES * PEER_KEYS
    nst = seq // tm
    qdim = q_w.shape[1]
    qwt = q_w.T.astype(BF16)
    u = expert_u.astype(BF16)
    vt = expert_v.T.astype(BF16)
    const2 = lambda i, e: (0, 0)
    const3 = lambda i, e: (0, 0, 0)
    vec = pl.BlockSpec((1, 1, d), lambda i, e: (i // nst, 0, 0))
    hk = (PEER_HEADS, PEER_KEYS, tm)
    return pl.pallas_call(
        functools.partial(_peer_kernel, tm=tm, me=me),
        grid=(n // tm, n_exp // me),
        in_specs=[pl.BlockSpec((tm, d), lambda i, e: (i, 0)),
                  pl.BlockSpec((1, d), const2), vec, vec, vec,
                  pl.BlockSpec((qdim, d), const2),
                  pl.BlockSpec((PEER_HEADS, PEER_KEYS, PEER_HALF), const3),
                  pl.BlockSpec((PEER_HEADS, PEER_KEYS, PEER_HALF), const3),
                  pl.BlockSpec((me, d), lambda i, e: (e, 0)),
                  pl.BlockSpec((d, me), lambda i, e: (0, e))],
        out_specs=pl.BlockSpec((tm, d), lambda i, e: (i, 0)),
        out_shape=jax.ShapeDtypeStruct((n, d), F32),
        scratch_shapes=[pltpu.VMEM((d, tm), BF16),
                        pltpu.VMEM((2 * PEER_HEADS, PEER_HALF, tm), BF16),
                        pltpu.VMEM(hk, F32),
                        pltpu.VMEM(hk, F32),
                        pltpu.VMEM(hk, F32),
                        pltpu.VMEM(hk, F32),
                        pltpu.VMEM((PEER_TOPK, tm), F32),
                        pltpu.VMEM((PEER_TOPK, tm), F32),
                        pltpu.VMEM((me, tm), F32),
                        pltpu.VMEM((me, tm), BF16),
                        pltpu.VMEM((d, tm), F32)],
        compiler_params=_cparams(("arbitrary", "arbitrary")),
        name="peer_dense",
    )(x2, g.reshape(1, d), shift, scale, gate, qwt, subkey1.astype(BF16), subkey2.astype(BF16), u, vt)


def _fox_proj_kernel(x_ref, g_ref, sh_ref, sc_ref, w_ref, wf_ref, fb_ref, qg_ref, kg_ref, bd_ref,
                     q_ref, k_ref, v_ref, og_ref, cumt_ref, cum_ref, carry_ref, *, tm, d):
    s = pl.program_id(1)

    @pl.when(s == 0)
    def _():
        carry_ref[...] = jnp.zeros_like(carry_ref)

    h = _norm_mod(x_ref[...], g_ref[...], sh_ref[0], sc_ref[0])
    hb = h.astype(BF16)
    y = jnp.dot(hb, w_ref[...], preferred_element_type=F32)
    bd = bd_ref[...]

    def head_norm(z, gain, post):
        z2 = z * z
        hi = z2.astype(BF16)
        lo = (z2 - hi.astype(F32)).astype(BF16)
        outs = []
        for t in range(d // LANES):
            sl = slice(t * LANES, (t + 1) * LANES)
            ms = (jnp.dot(hi[:, sl], bd, preferred_element_type=F32)
                  + jnp.dot(lo[:, sl], bd, preferred_element_type=F32))
            outs.append(z[:, sl] * lax.rsqrt(ms + NORM_EPS) * (gain[:, sl] * post))
        return jnp.concatenate(outs, axis=1)

    q_ref[...] = head_norm(y[:, 0:d], qg_ref[...], FOX_HEAD_DIM ** -0.5).astype(BF16)
    k_ref[...] = head_norm(y[:, d:2 * d], kg_ref[...], 1.0).astype(BF16)
    v_ref[...] = y[:, 2 * d:3 * d].astype(BF16)
    og_ref[...] = jax.nn.sigmoid(y[:, 3 * d:4 * d])

    lf = jnp.dot(hb, wf_ref[...], preferred_element_type=F32) + fb_ref[...]
    c = -_softplus(-lf)
    row = lax.broadcasted_iota(jnp.int32, (tm, LANES), 0)
    k = 1
    while k < tm:
        c = c + jnp.where(row >= k, pltpu.roll(c, k, axis=0), 0.0)
        k *= 2
    cum = c + carry_ref[...]
    carry_ref[...] = cum[tm - 1:tm, :]
    cum_ref[...] = cum
    cumt_ref[0] = cum.T[0:FOX_HEADS, :]


def _fox_attn_kernel(q_ref, k_ref, v_ref, cum_ref, cumt_ref, o_ref, *, tq):
    hp = pl.program_id(1)
    qi = pl.program_id(2)
    q = q_ref[...]
    cum = cum_ref[...]
    lane = lax.broadcasted_iota(jnp.int32, (tq, LANES), 1)
    rowi = lax.broadcasted_iota(jnp.int32, (tq, tq), 0)
    coli = lax.broadcasted_iota(jnp.int32, (tq, tq), 1)
    total = jnp.zeros((tq, LANES), F32)
    for hh in range(2):
        head = 2 * hp + hh
        mine = (lane < FOX_HEAD_DIM) if hh == 0 else (lane >= FOX_HEAD_DIM)
        qm = jnp.where(mine, q, jnp.zeros_like(q))
        cq = jnp.sum(jnp.where(lane == head, cum, 0.0), axis=1, keepdims=True)

        def chunk(j, carry, masked):
            m, l, acc = carry
            off = pl.multiple_of(j * tq, tq)
            kk = k_ref[pl.ds(off, tq), :]
            vv = v_ref[pl.ds(off, tq), :]
            s = lax.dot_general(qm, kk, (((1,), (1,)), ((), ())), preferred_element_type=F32)
            ck = cumt_ref[0, head, pl.ds(j, 1), :]
            s = s + (cq - ck)
            if masked:
                s = jnp.where(rowi >= coli, s, _NEG_INF)
            m_new = jnp.maximum(m, jnp.max(s, axis=1, keepdims=True))
            alpha = jnp.exp(m - m_new)
            p = jnp.exp(s - m_new)
            l_new = alpha * l + jnp.sum(p, axis=1, keepdims=True)
            vm = jnp.where(mine, vv, jnp.zeros_like(vv))
            acc_new = alpha * acc + jnp.dot(p.astype(BF16), vm, preferred_element_type=F32)
            return m_new, l_new, acc_new

        init = (jnp.full((tq, 1), _NEG_INF, F32), jnp.zeros((tq, 1), F32), jnp.zeros((tq, LANES), F32))
        carry = lax.fori_loop(0, qi, lambda j, cr: chunk(j, cr, False), init)
        m, l, acc = chunk(qi, carry, True)
        total = total + acc * (1.0 / l)
    o_ref[...] = total


def _fox_out_kernel(x_ref, o_ref_in, og_ref, gt_ref, w_ref, out_ref):
    y = (o_ref_in[...] * og_ref[...]).astype(BF16)
    out_ref[...] = x_ref[...] + gt_ref[0] * jnp.dot(y, w_ref[...], preferred_element_type=F32)


def _fox_layer(x2, n_batch, seq, g, shift, scale, gate, in_w, f_b, qg, kg, out_w):
    n, d = x2.shape
    nh = FOX_HEADS
    tm = 256
    nst = seq // tm
    w_main = jnp.concatenate([in_w[:, :3 * d], in_w[:, 3 * d + nh:]], axis=1).astype(BF16)
    wf = jnp.pad(in_w[:, 3 * d:3 * d + nh], ((0, 0), (0, LANES - nh))).astype(BF16)
    fb = jnp.pad(f_b, (0, LANES - nh)).reshape(1, LANES)
    qg_t = jnp.tile(qg, nh).reshape(1, d)
    kg_t = jnp.tile(kg, nh).reshape(1, d)
    ii = jnp.arange(LANES) // FOX_HEAD_DIM
    bd = jnp.where(ii[:, None] == ii[None, :], 1.0 / FOX_HEAD_DIM, 0.0).astype(BF16)
    const2 = lambda b, s: (0, 0)
    vec = pl.BlockSpec((1, 1, d), lambda b, s: (b, 0, 0))
    tok = pl.BlockSpec((tm, d), lambda b, s: (b * nst + s, 0))
    q, k, v, og, cumt, cum = pl.pallas_call(
        functools.partial(_fox_proj_kernel, tm=tm, d=d),
        grid=(n_batch, nst),
        in_specs=[tok, pl.BlockSpec((1, d), const2), vec, vec,
                  pl.BlockSpec((d, 4 * d), const2),
                  pl.BlockSpec((d, LANES), const2),
                  pl.BlockSpec((1, LANES), const2),
                  pl.BlockSpec((1, d), const2),
                  pl.BlockSpec((1, d), const2),
                  pl.BlockSpec((LANES, LANES), const2)],
        out_specs=[tok, tok, tok, tok,
                   pl.BlockSpec((1, nh, tm), lambda b, s: (b, 0, s)),
                   pl.BlockSpec((tm, LANES), lambda b, s: (b * nst + s, 0))],
        out_shape=[jax.ShapeDtypeStruct((n, d), BF16), jax.ShapeDtypeStruct((n, d), BF16),
                   jax.ShapeDtypeStruct((n, d), BF16), jax.ShapeDtypeStruct((n, d), F32),
                   jax.ShapeDtypeStruct((n_batch, nh, seq), F32),
                   jax.ShapeDtypeStruct((n, LANES), F32)],
        scratch_shapes=[pltpu.VMEM((1, LANES), F32)],
        compiler_params=_cparams(("arbitrary", "arbitrary")),
        name="fox_proj",
    )(x2, g.reshape(1, d), shift, scale, w_main, wf, fb, qg_t, kg_t, bd)

    tq = 256
    nq = seq // tq
    cumt4 = cumt.reshape(n_batch, nh, nq, tq)
    qblk = pl.BlockSpec((tq, LANES), lambda b, hp, qi: (b * nq + qi, hp))
    kvblk = pl.BlockSpec((seq, LANES), lambda b, hp, qi: (b, hp))
    o = pl.pallas_call(
        functools.partial(_fox_attn_kernel, tq=tq),
        grid=(n_batch, nh // 2, nq),
        in_specs=[qblk, kvblk, kvblk,
                  pl.BlockSpec((tq, LANES), lambda b, hp, qi: (b * nq + qi, 0)),
                  pl.BlockSpec((1, nh, nq, tq), lambda b, hp, qi: (b, 0, 0, 0))],
        out_specs=qblk,
        out_shape=jax.ShapeDtypeStruct((n, d), F32),
        compiler_params=_cparams(("arbitrary", "arbitrary", "arbitrary")),
        name="fox_attn",
    )(q, k, v, cum, cumt4)

    to = 512
    tok2 = pl.BlockSpec((to, d), lambda i: (i, 0))
    return pl.pallas_call(
        _fox_out_kernel,
        grid=(n // to,),
        in_specs=[tok2, tok2, tok2,
                  pl.BlockSpec((1, 1, d), lambda i: (i // (seq // to), 0, 0)),
                  pl.BlockSpec((d, d), lambda i: (0, 0))],
        out_specs=tok2,
        out_shape=jax.ShapeDtypeStruct((n, d), F32),
        compiler_params=_cparams(("arbitrary",)),
        name="fox_out",
    )(x2, o, og, gate, out_w.astype(BF16))


def kernel(x, c, l0_mix_norm_g, l0_mix_mod_w, l0_mix_mod_b, l0_lru_in_w, l0_lru_conv_w, l0_lru_conv_b, l0_lru_ra_w, l0_lru_ra_b, l0_lru_ri_w, l0_lru_ri_b, l0_lru_lambda, l0_lru_out_w, l0_ffn_norm_g, l0_ffn_mod_w, l0_ffn_mod_b, l0_peer_q_w, l0_peer_subkey1, l0_peer_subkey2, l0_peer_u, l0_peer_v, l1_mix_norm_g, l1_mix_mod_w, l1_mix_mod_b, l1_fox_in_w, l1_fox_f_b, l1_fox_q_norm_g, l1_fox_k_norm_g, l1_fox_out_w, l1_ffn_norm_g, l1_ffn_mod_w, l1_ffn_mod_b, l1_peer_q_w, l1_peer_subkey1, l1_peer_subkey2, l1_peer_u, l1_peer_v):
    n_batch, seq, d = x.shape
    x2 = x.reshape(n_batch * seq, d)
    c_pad = jnp.pad(c, ((0, SUBLANES - n_batch % SUBLANES), (0, 0))) if n_batch % SUBLANES else c

    sh, sc, gt = _ada_mod(c_pad, l0_mix_mod_w, l0_mix_mod_b, n_batch)
    x2 = _lru_layer(x2, n_batch, seq, l0_mix_norm_g, sh, sc, gt, l0_lru_in_w, l0_lru_conv_w, l0_lru_conv_b,
                    l0_lru_ra_w, l0_lru_ra_b, l0_lru_ri_w, l0_lru_ri_b, l0_lru_lambda, l0_lru_out_w)
    sh, sc, gt = _ada_mod(c_pad, l0_ffn_mod_w, l0_ffn_mod_b, n_batch)
    x2 = _peer_layer(x2, n_batch, seq, l0_ffn_norm_g, sh, sc, gt, l0_peer_q_w, l0_peer_subkey1, l0_peer_subkey2,
                     l0_peer_u, l0_peer_v)
    sh, sc, gt = _ada_mod(c_pad, l1_mix_mod_w, l1_mix_mod_b, n_batch)
    x2 = _fox_layer(x2, n_batch, seq, l1_mix_norm_g, sh, sc, gt, l1_fox_in_w, l1_fox_f_b, l1_fox_q_norm_g,
                    l1_fox_k_norm_g, l1_fox_out_w)
    sh, sc, gt = _ada_mod(c_pad, l1_ffn_mod_w, l1_ffn_mod_b, n_batch)
    x2 = _peer_layer(x2, n_batch, seq, l1_ffn_norm_g, sh, sc, gt, l1_peer_q_w, l1_peer_subkey1, l1_peer_subkey2,
                     l1_peer_u, l1_peer_v)
    return x2.reshape(n_batch, seq, d)
```

```python
import functools

import jax
import jax.numpy as jnp
from jax import lax
from jax.experimental import pallas as pl
from jax.experimental.pallas import tpu as pltpu

F32 = jnp.float32
BF16 = jnp.bfloat16

NORM_EPS = 1e-6
LRU_BLOCKS = 16
LRU_C = 8.0
CONV_WIDTH = 4
FOX_HEADS = 16
FOX_HEAD_DIM = 64
PEER_HEADS = 8
PEER_KEYS = 128
PEER_TOPK = 16
PEER_HALF = 128

LANES = 128
SUBLANES = 8
BF16_ROWS = 16
VMEM_LIMIT = 56 * 1024 * 1024

_SQRT_HALF = 0.7071067811865476
_NEG_INF = float("-inf")


def _gelu(x):
    return 0.5 * x * (1.0 + lax.erf(x * _SQRT_HALF))


def _softplus(z):
    return jnp.maximum(z, 0.0) + jnp.log1p(jnp.exp(-jnp.abs(z)))


def _norm_mod(x, g, shift, scale):
    ms = jnp.mean(x * x, axis=-1, keepdims=True)
    y = x * lax.rsqrt(ms + NORM_EPS) * g
    return y * (1.0 + scale) + shift


def _cparams(sem, flags=None):
    return pltpu.CompilerParams(dimension_semantics=sem, vmem_limit_bytes=VMEM_LIMIT, flags=flags)


def _mod_kernel(c_ref, w_ref, b_ref, o_ref):
    c = c_ref[...]
    s = c * jax.nn.sigmoid(c)
    o_ref[...] = jnp.dot(s.astype(BF16), w_ref[...].astype(BF16), preferred_element_type=F32) + b_ref[...]


def _ada_mod(c_pad, w, b, n_batch):
    d, d3 = w.shape
    tn = 512
    mod = pl.pallas_call(
        _mod_kernel,
        grid=(d3 // tn,),
        in_specs=[pl.BlockSpec(c_pad.shape, lambda j: (0, 0)),
                  pl.BlockSpec((d, tn), lambda j: (0, j)),
                  pl.BlockSpec((1, tn), lambda j: (0, j))],
        out_specs=pl.BlockSpec((c_pad.shape[0], tn), lambda j: (0, j)),
        out_shape=jax.ShapeDtypeStruct((c_pad.shape[0], d3), F32),
        compiler_params=_cparams(("arbitrary",)),
        name="ada_mod",
    )(c_pad, w, b.reshape(1, d3))
    mod = mod[:n_batch]
    shift, scale, gate = jnp.split(mod, 3, axis=-1)
    return shift[:, None, :], scale[:, None, :], gate[:, None, :]


def _lru_kernel(x_ref, g_ref, sh_ref, sc_ref, gt_ref, inw_ref, cw_ref, cb_ref, gw_ref, gb_ref, lam_ref, ow_ref,
                o_ref, y3_ref, xc3_ref, xcb_ref, gz3_ref, yob_ref, xprev_ref, hprev_ref, *, tb, nct):
    s = pl.program_id(1)

    @pl.when(s == 0)
    def _():
        xprev_ref[...] = jnp.zeros_like(xprev_ref)
        hprev_ref[...] = jnp.zeros_like(hprev_ref)

    x = x_ref[...]
    h = _norm_mod(x, g_ref[...], sh_ref[0], sc_ref[0])
    y = jnp.dot(h.astype(BF16), inw_ref[...], preferred_element_type=F32)
    for c in range(2 * nct):
        y3_ref[c] = y[:, c * LANES:(c + 1) * LANES]

    for c in range(nct):
        xb = y3_ref[c]
        full = jnp.concatenate([xprev_ref[c], xb], axis=0)
        xprev_ref[c] = xb[tb - SUBLANES:, :]
        cw = cw_ref[c]
        xc = cb_ref[c] + cw[CONV_WIDTH - 1:CONV_WIDTH] * xb
        for j in range(1, CONV_WIDTH):
            shifted = pltpu.roll(full, j, axis=0)[SUBLANES:, :]
            xc = xc + cw[CONV_WIDTH - 1 - j:CONV_WIDTH - j] * shifted
        xc3_ref[c] = xc
        xcb_ref[:, c * LANES:(c + 1) * LANES] = xc.astype(BF16)

    gz = jnp.dot(xcb_ref[...], gw_ref[...], preferred_element_type=F32)
    for c in range(2 * nct):
        gz3_ref[c] = gz[:, c * LANES:(c + 1) * LANES]

    row = lax.broadcasted_iota(jnp.int32, (tb, LANES), 0)

    def tile_body(c, carry):
        xc = xc3_ref[c]
        r = jax.nn.sigmoid(gz3_ref[c] + gb_ref[c])
        i = jax.nn.sigmoid(gz3_ref[nct + c] + gb_ref[nct + c])
        log_a = (-LRU_C * _softplus(-lam_ref[c])) * r
        a = jnp.exp(log_a)
        u = jnp.sqrt(1.0 - jnp.exp(2.0 * log_a)) * (i * xc)
        k = 1
        while k < tb:
            keep = row >= k
            a_sh = jnp.where(keep, pltpu.roll(a, k, axis=0), 1.0)
            u_sh = jnp.where(keep, pltpu.roll(u, k, axis=0), 0.0)
            u = a * u_sh + u
            a = a * a_sh
            k *= 2
        hs = u + a * hprev_ref[c]
        hprev_ref[c] = hs[tb - 1:tb, :]
        y3_ref[c] = hs * _gelu(y3_ref[nct + c])
        return carry

    lax.fori_loop(0, nct, tile_body, 0)

    for c in range(nct):
        yob_ref[:, c * LANES:(c + 1) * LANES] = y3_ref[c].astype(BF16)
    out = jnp.dot(yob_ref[...], ow_ref[...], preferred_element_type=F32)
    o_ref[...] = x + gt_ref[0] * out


def _lru_layer(x2, n_batch, seq, g, shift, scale, gate, in_w, conv_w, conv_b, ra_w, ra_b, ri_w, ri_b, lam, out_w):
    n, d = x2.shape
    c_rnn = conv_w.shape[1]
    nct = c_rnn // LANES
    tb = 256
    nsb = seq // tb
    eye = jnp.eye(LRU_BLOCKS, dtype=F32)
    ra_d = jnp.einsum('nm,ncd->ncmd', eye, ra_w).reshape(c_rnn, c_rnn)
    ri_d = jnp.einsum('nm,ncd->ncmd', eye, ri_w).reshape(c_rnn, c_rnn)
    gw = jnp.concatenate([ra_d, ri_d], axis=1).astype(BF16)
    gb = jnp.concatenate([ra_b, ri_b]).reshape(2 * nct, 1, LANES)
    cw = conv_w.reshape(CONV_WIDTH, nct, LANES).transpose(1, 0, 2)
    cb = conv_b.reshape(nct, 1, LANES)
    lam3 = lam.reshape(nct, 1, LANES)
    const2 = lambda b, s: (0, 0)
    const3 = lambda b, s: (0, 0, 0)
    vec = pl.BlockSpec((1, 1, d), lambda b, s: (b, 0, 0))
    return pl.pallas_call(
        functools.partial(_lru_kernel, tb=tb, nct=nct),
        grid=(n_batch, nsb),
        in_specs=[pl.BlockSpec((tb, d), lambda b, s: (b * nsb + s, 0)),
                  pl.BlockSpec((1, d), const2), vec, vec, vec,
                  pl.BlockSpec((d, 2 * c_rnn), const2),
                  pl.BlockSpec((nct, CONV_WIDTH, LANES), const3),
                  pl.BlockSpec((nct, 1, LANES), const3),
                  pl.BlockSpec((c_rnn, 2 * c_rnn), const2),
                  pl.BlockSpec((2 * nct, 1, LANES), const3),
                  pl.BlockSpec((nct, 1, LANES), const3),
                  pl.BlockSpec((c_rnn, d), const2)],
        out_specs=pl.BlockSpec((tb, d), lambda b, s: (b * nsb + s, 0)),
        out_shape=jax.ShapeDtypeStruct((n, d), F32),
        scratch_shapes=[pltpu.VMEM((2 * nct, tb, LANES), F32),
                        pltpu.VMEM((nct, tb, LANES), F32),
                        pltpu.VMEM((tb, c_rnn), BF16),
                        pltpu.VMEM((2 * nct, tb, LANES), F32),
                        pltpu.VMEM((tb, c_rnn), BF16),
                        pltpu.VMEM((nct, SUBLANES, LANES), F32),
                        pltpu.VMEM((nct, 1, LANES), F32)],
        compiler_params=_cparams(("arbitrary", "arbitrary")),
        name="rglru_mixer",
    )(x2, g.reshape(1, d), shift, scale, gate, in_w.astype(BF16), cw, cb, gw, gb, lam3, out_w.astype(BF16))


_UNRANKED = 99.0


def _top16_rows(cur, dst_ref, want_rank):
    rank = jnp.full(cur.shape, _UNRANKED, F32) if want_rank else None
    m = None
    for r in range(PEER_TOPK):
        m = jnp.max(cur, axis=0, keepdims=True)
        if dst_ref is not None:
            dst_ref[r:r + 1, :] = m
        hit = cur == m
        if want_rank:
            rank = jnp.where(hit, float(r), rank)
        if r + 1 < PEER_TOPK:
            cur = jnp.where(hit, _NEG_INF, cur)
    return rank, m


def _peer_select(s1, s2, v1_ref, v2_ref):
    rank1, _ = _top16_rows(s1, v1_ref, True)
    rank2, _ = _top16_rows(s2, v2_ref, True)
    row8 = lax.broadcasted_iota(jnp.int32, (SUBLANES, LANES), 0)
    blocks = [v1_ref[0:1, :] + v2_ref[...]]
    for i in range(1, SUBLANES):
        blk = v1_ref[i:i + 1, :] + v2_ref[0:SUBLANES, :]
        blocks.append(jnp.where(row8 < PEER_TOPK // (i + 1), blk, _NEG_INF))
    blocks.append(v1_ref[SUBLANES:PEER_TOPK, :] + v2_ref[0:1, :])
    cand = jnp.concatenate(blocks, axis=0)
    _, tau = _top16_rows(cand, None, False)
    m1 = v1_ref[0:1, :]
    m2 = v2_ref[0:1, :]
    z = jnp.sum(jnp.where(cand >= tau, jnp.exp(cand - (m1 + m2)), 0.0), axis=0, keepdims=True)
    cnt = jnp.zeros(s1.shape, F32)
    for i in range(SUBLANES):
        cnt_i = jnp.sum(jnp.where(blocks[i] >= tau, 1.0, 0.0), axis=0, keepdims=True)
        cnt = jnp.where(rank1 == float(i), cnt_i, cnt)
    cnt_hi = jnp.where(blocks[SUBLANES] >= tau, 1.0, 0.0)
    for r in range(SUBLANES):
        cnt = jnp.where(rank1 == float(SUBLANES + r), cnt_hi[r:r + 1, :], cnt)
    return cnt, jnp.exp(s1 - m1) * (1.0 / z), rank2, jnp.exp(s2 - m2)


def _peer_kernel(x_ref, g_ref, sh_ref, sc_ref, gt_ref, qwt_ref, k1_ref, k2_ref, u_ref, vt_ref, o_ref,
                 ht_ref, qt_ref, r2_ref, p2_ref, cnt_ref, p1_ref, v1_ref, v2_ref, acc_ref, bc_ref, bp_ref,
                 *, tm, me):
    e = pl.program_id(1)
    n_e = pl.num_programs(1)
    nchunk = tm // LANES
    nslab = me // PEER_KEYS
    half = PEER_KEYS // 2

    @pl.when(e == 0)
    def _prologue():
        h = _norm_mod(x_ref[...], g_ref[...], sh_ref[0], sc_ref[0])
        ht = h.T.astype(BF16)
        ht_ref[...] = ht
        qt = jnp.dot(qwt_ref[...], ht, preferred_element_type=F32)
        for i in range(2 * PEER_HEADS):
            qt_ref[i] = qt[i * PEER_HALF:(i + 1) * PEER_HALF, :].astype(BF16)
        acc_ref[...] = jnp.zeros_like(acc_ref)

        def head_body(hh, carry):
            s1 = jnp.dot(k1_ref[hh], qt_ref[2 * hh], preferred_element_type=F32)
            s2 = jnp.dot(k2_ref[hh], qt_ref[2 * hh + 1], preferred_element_type=F32)
            for c in range(nchunk):
                lanes = slice(c * LANES, (c + 1) * LANES)
                cnt, p1, rank2, p2 = _peer_select(s1[:, lanes], s2[:, lanes], v1_ref.at[c], v2_ref.at[c])
                cnt_ref[hh, :, lanes] = cnt
                p1_ref[hh, :, lanes] = p1
                r2_ref[hh, :, lanes] = rank2.astype(BF16)
                p2_ref[hh, :, lanes] = p2.astype(BF16)
            return carry

        lax.fori_loop(0, PEER_HEADS, head_body, 0)

    e1_0 = pl.multiple_of(e * nslab, SUBLANES)
    for hh in range(PEER_HEADS):
        cnt8 = cnt_ref[hh, pl.ds(e1_0, nslab), :]
        p18 = p1_ref[hh, pl.ds(e1_0, nslab), :]
        for sl in range(nslab):
            bc_ref[sl, hh] = jnp.broadcast_to(cnt8[sl:sl + 1, :], (BF16_ROWS, tm)).astype(BF16)
            bp_ref[sl, hh] = jnp.broadcast_to(p18[sl:sl + 1, :], (BF16_ROWS, tm)).astype(BF16)

    group = 2
    reps = half // BF16_ROWS
    for p in range(nslab // group):
        prow = slice(p * group * PEER_KEYS, (p + 1) * group * PEER_KEYS)
        a_grp = jnp.dot(u_ref[prow, :], ht_ref[...], preferred_element_type=F32)
        w_cols = []
        for c in range(nchunk):
            lanes = slice(c * LANES, (c + 1) * LANES)
            w_rows = []
            for s_in in range(group):
                sl = p * group + s_in
                for hf in range(2):
                    krows = slice(hf * half, (hf + 1) * half)
                    a = a_grp[s_in * PEER_KEYS + hf * half:s_in * PEER_KEYS + (hf + 1) * half, lanes]
                    gsum = jnp.zeros((half, LANES), BF16)
                    zero = jnp.zeros((half, LANES), BF16)
                    for hh in range(PEER_HEADS):
                        rank2 = r2_ref[hh, krows, lanes]
                        p2 = p2_ref[hh, krows, lanes]
                        cnt = pltpu.repeat(bc_ref[sl, hh, :, lanes], reps, axis=0)
                        p1 = pltpu.repeat(bp_ref[sl, hh, :, lanes], reps, axis=0)
                        gsum = gsum + jnp.where(rank2 < cnt, p2, zero) * p1
                    w_rows.append(gsum * _gelu(a).astype(BF16))
            w_cols.append(jnp.concatenate(w_rows, axis=0))
        w_grp = jnp.concatenate(w_cols, axis=1)
        acc_ref[...] += jnp.dot(vt_ref[:, prow], w_grp, preferred_element_type=F32)

    @pl.when(e == n_e - 1)
    def _fin():
        o_ref[...] = x_ref[...] + gt_ref[0] * acc_ref[...].T


def _peer_layer(x2, n_batch, seq, g, shift, scale, gate, q_w, subkey1, subkey2, expert_u, expert_v):
    n, d = x2.shape
    n_exp = expert_u.shape[0]
    tm = min(512, seq)
    me = SUBLANES * PEER_KEYS
    nst = seq // tm
    qdim = q_w.shape[1]
    qwt = q_w.T.astype(BF16)
    u = expert_u.astype(BF16)
    vt = expert_v.T.astype(BF16)
    const2 = lambda i, e: (0, 0)
    const3 = lambda i, e: (0, 0, 0)
    vec = pl.BlockSpec((1, 1, d), lambda i, e: (i // nst, 0, 0))
    hk = (PEER_HEADS, PEER_KEYS, tm)
    return pl.pallas_call(
        functools.partial(_peer_kernel, tm=tm, me=me),
        grid=(n // tm, n_exp // me),
        in_specs=[pl.BlockSpec((tm, d), lambda i, e: (i, 0)),
                  pl.BlockSpec((1, d), const2), vec, vec, vec,
                  pl.BlockSpec((qdim, d), const2),
                  pl.BlockSpec((PEER_HEADS, PEER_KEYS, PEER_HALF), const3),
                  pl.BlockSpec((PEER_HEADS, PEER_KEYS, PEER_HALF), const3),
                  pl.BlockSpec((me, d), lambda i, e: (e, 0)),
                  pl.BlockSpec((d, me), lambda i, e: (0, e))],
        out_specs=pl.BlockSpec((tm, d), lambda i, e: (i, 0)),
        out_shape=jax.ShapeDtypeStruct((n, d), F32),
        scratch_shapes=[pltpu.VMEM((d, tm), BF16),
                        pltpu.VMEM((2 * PEER_HEADS, PEER_HALF, tm), BF16),
                        pltpu.VMEM(hk, BF16),
                        pltpu.VMEM(hk, BF16),
                        pltpu.VMEM(hk, F32),
                        pltpu.VMEM(hk, F32),
                        pltpu.VMEM((tm // LANES, PEER_TOPK, LANES), F32),
                        pltpu.VMEM((tm // LANES, PEER_TOPK, LANES), F32),
                        pltpu.VMEM((d, tm), F32),
                        pltpu.VMEM((me // PEER_KEYS, PEER_HEADS, BF16_ROWS, tm), BF16),
                        pltpu.VMEM((me // PEER_KEYS, PEER_HEADS, BF16_ROWS, tm), BF16)],
        compiler_params=_cparams(("arbitrary", "arbitrary")),
        name="peer_dense",
    )(x2, g.reshape(1, d), shift, scale, gate, qwt, subkey1.astype(BF16), subkey2.astype(BF16), u, vt)


def _fox_proj_kernel(x_ref, g_ref, sh_ref, sc_ref, w_ref, wf_ref, fb_ref, qg_ref, kg_ref, bd_ref,
                     q_ref, k_ref, v_ref, og_ref, cumt_ref, cum_ref, carry_ref, *, tm, d):
    s = pl.program_id(1)

    @pl.when(s == 0)
    def _():
        carry_ref[...] = jnp.zeros_like(carry_ref)

    h = _norm_mod(x_ref[...], g_ref[...], sh_ref[0], sc_ref[0])
    hb = h.astype(BF16)
    y = jnp.dot(hb, w_ref[...], preferred_element_type=F32)
    bd = bd_ref[...]

    def head_norm(z, gain, post):
        z2 = z * z
        hi = z2.astype(BF16)
        lo = (z2 - hi.astype(F32)).astype(BF16)
        outs = []
        for t in range(d // LANES):
            sl = slice(t * LANES, (t + 1) * LANES)
            ms = (jnp.dot(hi[:, sl], bd, preferred_element_type=F32)
                  + jnp.dot(lo[:, sl], bd, preferred_element_type=F32))
            outs.append(z[:, sl] * lax.rsqrt(ms + NORM_EPS) * (gain[:, sl] * post))
        return jnp.concatenate(outs, axis=1)

    q_ref[...] = head_norm(y[:, 0:d], qg_ref[...], FOX_HEAD_DIM ** -0.5).astype(BF16)
    k_ref[...] = head_norm(y[:, d:2 * d], kg_ref[...], 1.0).astype(BF16)
    v_ref[...] = y[:, 2 * d:3 * d].astype(BF16)
    og_ref[...] = jax.nn.sigmoid(y[:, 3 * d:4 * d])

    lf = jnp.dot(hb, wf_ref[...], preferred_element_type=F32) + fb_ref[...]
    c = -_softplus(-lf)
    row = lax.broadcasted_iota(jnp.int32, (tm, LANES), 0)
    k = 1
    while k < tm:
        c = c + jnp.where(row >= k, pltpu.roll(c, k, axis=0), 0.0)
        k *= 2
    cum = c + carry_ref[...]
    carry_ref[...] = cum[tm - 1:tm, :]
    cum_ref[...] = cum
    cumt_ref[0] = cum.T[0:FOX_HEADS, :]


def _fox_attn_kernel(q_ref, k_ref, v_ref, cum_ref, cumt_ref, o_ref, *, tq):
    hp = pl.program_id(1)
    qi = pl.program_id(2)
    q = q_ref[...]
    cum = cum_ref[...]
    lane = lax.broadcasted_iota(jnp.int32, (tq, LANES), 1)
    rowi = lax.broadcasted_iota(jnp.int32, (tq, tq), 0)
    coli = lax.broadcasted_iota(jnp.int32, (tq, tq), 1)
    total = jnp.zeros((tq, LANES), F32)
    for hh in range(2):
        head = 2 * hp + hh
        mine = (lane < FOX_HEAD_DIM) if hh == 0 else (lane >= FOX_HEAD_DIM)
        qm = jnp.where(mine, q, jnp.zeros_like(q))
        cq = jnp.sum(jnp.where(lane == head, cum, 0.0), axis=1, keepdims=True)

        def chunk(j, carry, masked):
            m, l, acc = carry
            off = pl.multiple_of(j * tq, tq)
            kk = k_ref[pl.ds(off, tq), :]
            vv = v_ref[pl.ds(off, tq), :]
            s = lax.dot_general(qm, kk, (((1,), (1,)), ((), ())), preferred_element_type=F32)
            ck = cumt_ref[0, head, pl.ds(j, 1), :]
            s = s + (cq - ck)
            if masked:
                s = jnp.where(rowi >= coli, s, _NEG_INF)
            m_new = jnp.maximum(m, jnp.max(s, axis=1, keepdims=True))
            alpha = jnp.exp(m - m_new)
            p = jnp.exp(s - m_new)
            l_new = alpha * l + jnp.sum(p, axis=1, keepdims=True)
            vm = jnp.where(mine, vv, jnp.zeros_like(vv))
            acc_new = alpha * acc + jnp.dot(p.astype(BF16), vm, preferred_element_type=F32)
            return m_new, l_new, acc_new

        init = (jnp.full((tq, 1), _NEG_INF, F32), jnp.zeros((tq, 1), F32), jnp.zeros((tq, LANES), F32))
        carry = lax.fori_loop(0, qi, lambda j, cr: chunk(j, cr, False), init)
        m, l, acc = chunk(qi, carry, True)
        total = total + acc * (1.0 / l)
    o_ref[...] = total


def _fox_out_kernel(x_ref, o_ref_in, og_ref, gt_ref, w_ref, out_ref):
    y = (o_ref_in[...] * og_ref[...]).astype(BF16)
    out_ref[...] = x_ref[...] + gt_ref[0] * jnp.dot(y, w_ref[...], preferred_element_type=F32)


def _fox_layer(x2, n_batch, seq, g, shift, scale, gate, in_w, f_b, qg, kg, out_w):
    n, d = x2.shape
    nh = FOX_HEADS
    tm = 256
    nst = seq // tm
    w_main = jnp.concatenate([in_w[:, :3 * d], in_w[:, 3 * d + nh:]], axis=1).astype(BF16)
    wf = jnp.pad(in_w[:, 3 * d:3 * d + nh], ((0, 0), (0, LANES - nh))).astype(BF16)
    fb = jnp.pad(f_b, (0, LANES - nh)).reshape(1, LANES)
    qg_t = jnp.tile(qg, nh).reshape(1, d)
    kg_t = jnp.tile(kg, nh).reshape(1, d)
    ii = jnp.arange(LANES) // FOX_HEAD_DIM
    bd = jnp.where(ii[:, None] == ii[None, :], 1.0 / FOX_HEAD_DIM, 0.0).astype(BF16)
    const2 = lambda b, s: (0, 0)
    vec = pl.BlockSpec((1, 1, d), lambda b, s: (b, 0, 0))
    tok = pl.BlockSpec((tm, d), lambda b, s: (b * nst + s, 0))
    q, k, v, og, cumt, cum = pl.pallas_call(
        functools.partial(_fox_proj_kernel, tm=tm, d=d),
        grid=(n_batch, nst),
        in_specs=[tok, pl.BlockSpec((1, d), const2), vec, vec,
                  pl.BlockSpec((d, 4 * d), const2),
                  pl.BlockSpec((d, LANES), const2),
                  pl.BlockSpec((1, LANES), const2),
                  pl.BlockSpec((1, d), const2),
                  pl.BlockSpec((1, d), const2),
                  pl.BlockSpec((LANES, LANES), const2)],
        out_specs=[tok, tok, tok, tok,
                   pl.BlockSpec((1, nh, tm), lambda b, s: (b, 0, s)),
                   pl.BlockSpec((tm, LANES), lambda b, s: (b * nst + s, 0))],
        out_shape=[jax.ShapeDtypeStruct((n, d), BF16), jax.ShapeDtypeStruct((n, d), BF16),
                   jax.ShapeDtypeStruct((n, d), BF16), jax.ShapeDtypeStruct((n, d), F32),
                   jax.ShapeDtypeStruct((n_batch, nh, seq), F32),
                   jax.ShapeDtypeStruct((n, LANES), F32)],
        scratch_shapes=[pltpu.VMEM((1, LANES), F32)],
        compiler_params=_cparams(("arbitrary", "arbitrary")),
        name="fox_proj",
    )(x2, g.reshape(1, d), shift, scale, w_main, wf, fb, qg_t, kg_t, bd)

    tq = 256
    nq = seq // tq
    cumt4 = cumt.reshape(n_batch, nh, nq, tq)
    qblk = pl.BlockSpec((tq, LANES), lambda b, hp, qi: (b * nq + qi, hp))
    kvblk = pl.BlockSpec((seq, LANES), lambda b, hp, qi: (b, hp))
    o = pl.pallas_call(
        functools.partial(_fox_attn_kernel, tq=tq),
        grid=(n_batch, nh // 2, nq),
        in_specs=[qblk, kvblk, kvblk,
                  pl.BlockSpec((tq, LANES), lambda b, hp, qi: (b * nq + qi, 0)),
                  pl.BlockSpec((1, nh, nq, tq), lambda b, hp, qi: (b, 0, 0, 0))],
        out_specs=qblk,
        out_shape=jax.ShapeDtypeStruct((n, d), F32),
        compiler_params=_cparams(("arbitrary", "arbitrary", "arbitrary")),
        name="fox_attn",
    )(q, k, v, cum, cumt4)

    to = 512
    tok2 = pl.BlockSpec((to, d), lambda i: (i, 0))
    return pl.pallas_call(
        _fox_out_kernel,
        grid=(n // to,),
        in_specs=[tok2, tok2, tok2,
                  pl.BlockSpec((1, 1, d), lambda i: (i // (seq // to), 0, 0)),
                  pl.BlockSpec((d, d), lambda i: (0, 0))],
        out_specs=tok2,
        out_shape=jax.ShapeDtypeStruct((n, d), F32),
        compiler_params=_cparams(("arbitrary",)),
        name="fox_out",
    )(x2, o, og, gate, out_w.astype(BF16))


def kernel(x, c, l0_mix_norm_g, l0_mix_mod_w, l0_mix_mod_b, l0_lru_in_w, l0_lru_conv_w, l0_lru_conv_b, l0_lru_ra_w, l0_lru_ra_b, l0_lru_ri_w, l0_lru_ri_b, l0_lru_lambda, l0_lru_out_w, l0_ffn_norm_g, l0_ffn_mod_w, l0_ffn_mod_b, l0_peer_q_w, l0_peer_subkey1, l0_peer_subkey2, l0_peer_u, l0_peer_v, l1_mix_norm_g, l1_mix_mod_w, l1_mix_mod_b, l1_fox_in_w, l1_fox_f_b, l1_fox_q_norm_g, l1_fox_k_norm_g, l1_fox_out_w, l1_ffn_norm_g, l1_ffn_mod_w, l1_ffn_mod_b, l1_peer_q_w, l1_peer_subkey1, l1_peer_subkey2, l1_peer_u, l1_peer_v):
    n_batch, seq, d = x.shape
    x2 = x.reshape(n_batch * seq, d)
    c_pad = jnp.pad(c, ((0, SUBLANES - n_batch % SUBLANES), (0, 0))) if n_batch % SUBLANES else c

    sh, sc, gt = _ada_mod(c_pad, l0_mix_mod_w, l0_mix_mod_b, n_batch)
    x2 = _lru_layer(x2, n_batch, seq, l0_mix_norm_g, sh, sc, gt, l0_lru_in_w, l0_lru_conv_w, l0_lru_conv_b,
                    l0_lru_ra_w, l0_lru_ra_b, l0_lru_ri_w, l0_lru_ri_b, l0_lru_lambda, l0_lru_out_w)
    sh, sc, gt = _ada_mod(c_pad, l0_ffn_mod_w, l0_ffn_mod_b, n_batch)
    x2 = _peer_layer(x2, n_batch, seq, l0_ffn_norm_g, sh, sc, gt, l0_peer_q_w, l0_peer_subkey1, l0_peer_subkey2,
                     l0_peer_u, l0_peer_v)
    sh, sc, gt = _ada_mod(c_pad, l1_mix_mod_w, l1_mix_mod_b, n_batch)
    x2 = _fox_layer(x2, n_batch, seq, l1_mix_norm_g, sh, sc, gt, l1_fox_in_w, l1_fox_f_b, l1_fox_q_norm_g,
                    l1_fox_k_norm_g, l1_fox_out_w)
    sh, sc, gt = _ada_mod(c_pad, l1_ffn_mod_w, l1_ffn_mod_b, n_batch)
    x2 = _peer_layer(x2, n_batch, seq, l1_ffn_norm_g, sh, sc, gt, l1_peer_q_w, l1_peer_subkey1, l1_peer_subkey2,
                     l1_peer_u, l1_peer_v)
    return x2.reshape(n_batch, seq, d)
```

```python
import functools

import jax
import jax.numpy as jnp
from jax import lax
from jax.experimental import pallas as pl
from jax.experimental.pallas import tpu as pltpu

F32 = jnp.float32
BF16 = jnp.bfloat16

NORM_EPS = 1e-6
LRU_BLOCKS = 16
LRU_C = 8.0
CONV_WIDTH = 4
FOX_HEADS = 16
FOX_HEAD_DIM = 64
FOX_TK = 128
FOX_CK_ROWS = 512
FOX_UNROLL = 2
PEER_HEADS = 8
PEER_KEYS = 128
PEER_TOPK = 16
PEER_HALF = 128

LANES = 128
SUBLANES = 8
GATE_DT = jnp.float32
GATE_VROWS = 8
GATE_ROWS = 32
VMEM_LIMIT = 56 * 1024 * 1024

_SQRT_HALF = 0.7071067811865476
_NEG_INF = float("-inf")


def _gelu(x):
    return 0.5 * x * (1.0 + lax.erf(x * _SQRT_HALF))


def _softplus(z):
    return jnp.maximum(z, 0.0) + jnp.log1p(jnp.exp(-jnp.abs(z)))


def _norm_mod(x, g, shift, scale):
    ms = jnp.mean(x * x, axis=-1, keepdims=True)
    y = x * lax.rsqrt(ms + NORM_EPS) * g
    return y * (1.0 + scale) + shift


def _cparams(sem, flags=None):
    return pltpu.CompilerParams(dimension_semantics=sem, vmem_limit_bytes=VMEM_LIMIT, flags=flags)


def _mod_kernel(c_ref, w_ref, b_ref, o_ref):
    c = c_ref[...]
    s = c * jax.nn.sigmoid(c)
    o_ref[...] = jnp.dot(s.astype(BF16), w_ref[...].astype(BF16), preferred_element_type=F32) + b_ref[...]


def _ada_mod(c_pad, w, b, n_batch):
    d, d3 = w.shape
    tn = 512
    mod = pl.pallas_call(
        _mod_kernel,
        grid=(d3 // tn,),
        in_specs=[pl.BlockSpec(c_pad.shape, lambda j: (0, 0)),
                  pl.BlockSpec((d, tn), lambda j: (0, j)),
                  pl.BlockSpec((1, tn), lambda j: (0, j))],
        out_specs=pl.BlockSpec((c_pad.shape[0], tn), lambda j: (0, j)),
        out_shape=jax.ShapeDtypeStruct((c_pad.shape[0], d3), F32),
        compiler_params=_cparams(("arbitrary",)),
        name="ada_mod",
    )(c_pad, w, b.reshape(1, d3))
    mod = mod[:n_batch]
    shift, scale, gate = jnp.split(mod, 3, axis=-1)
    return shift[:, None, :], scale[:, None, :], gate[:, None, :]


def _lru_kernel(x_ref, g_ref, sh_ref, sc_ref, gt_ref, inw_ref, cw_ref, cb_ref, gw_ref, gb_ref, lam_ref, ow_ref,
                o_ref, y3_ref, xc3_ref, xcb_ref, gz3_ref, yob_ref, xprev_ref, hprev_ref, *, tb, nct):
    s = pl.program_id(1)

    @pl.when(s == 0)
    def _():
        xprev_ref[...] = jnp.zeros_like(xprev_ref)
        hprev_ref[...] = jnp.zeros_like(hprev_ref)

    x = x_ref[...]
    h = _norm_mod(x, g_ref[...], sh_ref[0], sc_ref[0])
    y = jnp.dot(h.astype(BF16), inw_ref[...], preferred_element_type=F32)
    for c in range(2 * nct):
        y3_ref[c] = y[:, c * LANES:(c + 1) * LANES]

    for c in range(nct):
        xb = y3_ref[c]
        full = jnp.concatenate([xprev_ref[c], xb], axis=0)
        xprev_ref[c] = xb[tb - SUBLANES:, :]
        cw = cw_ref[c]
        xc = cb_ref[c] + cw[CONV_WIDTH - 1:CONV_WIDTH] * xb
        for j in range(1, CONV_WIDTH):
            shifted = pltpu.roll(full, j, axis=0)[SUBLANES:, :]
            xc = xc + cw[CONV_WIDTH - 1 - j:CONV_WIDTH - j] * shifted
        xc3_ref[c] = xc
        xcb_ref[:, c * LANES:(c + 1) * LANES] = xc.astype(BF16)

    gz = jnp.dot(xcb_ref[...], gw_ref[...], preferred_element_type=F32)
    for c in range(2 * nct):
        gz3_ref[c] = gz[:, c * LANES:(c + 1) * LANES]

    row = lax.broadcasted_iota(jnp.int32, (tb, LANES), 0)

    def tile_body(c, carry):
        xc = xc3_ref[c]
        r = jax.nn.sigmoid(gz3_ref[c] + gb_ref[c])
        i = jax.nn.sigmoid(gz3_ref[nct + c] + gb_ref[nct + c])
        log_a = (-LRU_C * _softplus(-lam_ref[c])) * r
        a = jnp.exp(log_a)
        u = jnp.sqrt(1.0 - jnp.exp(2.0 * log_a)) * (i * xc)
        k = 1
        while k < tb:
            keep = row >= k
            a_sh = jnp.where(keep, pltpu.roll(a, k, axis=0), 1.0)
            u_sh = jnp.where(keep, pltpu.roll(u, k, axis=0), 0.0)
            u = a * u_sh + u
            a = a * a_sh
            k *= 2
        hs = u + a * hprev_ref[c]
        hprev_ref[c] = hs[tb - 1:tb, :]
        y3_ref[c] = hs * _gelu(y3_ref[nct + c])
        return carry

    lax.fori_loop(0, nct, tile_body, 0)

    for c in range(nct):
        yob_ref[:, c * LANES:(c + 1) * LANES] = y3_ref[c].astype(BF16)
    out = jnp.dot(yob_ref[...], ow_ref[...], preferred_element_type=F32)
    o_ref[...] = x + gt_ref[0] * out


def _lru_layer(x2, n_batch, seq, g, shift, scale, gate, in_w, conv_w, conv_b, ra_w, ra_b, ri_w, ri_b, lam, out_w):
    n, d = x2.shape
    c_rnn = conv_w.shape[1]
    nct = c_rnn // LANES
    tb = 256
    nsb = seq // tb
    eye = jnp.eye(LRU_BLOCKS, dtype=F32)
    ra_d = jnp.einsum('nm,ncd->ncmd', eye, ra_w).reshape(c_rnn, c_rnn)
    ri_d = jnp.einsum('nm,ncd->ncmd', eye, ri_w).reshape(c_rnn, c_rnn)
    gw = jnp.concatenate([ra_d, ri_d], axis=1).astype(BF16)
    gb = jnp.concatenate([ra_b, ri_b]).reshape(2 * nct, 1, LANES)
    cw = conv_w.reshape(CONV_WIDTH, nct, LANES).transpose(1, 0, 2)
    cb = conv_b.reshape(nct, 1, LANES)
    lam3 = lam.reshape(nct, 1, LANES)
    const2 = lambda b, s: (0, 0)
    const3 = lambda b, s: (0, 0, 0)
    vec = pl.BlockSpec((1, 1, d), lambda b, s: (b, 0, 0))
    return pl.pallas_call(
        functools.partial(_lru_kernel, tb=tb, nct=nct),
        grid=(n_batch, nsb),
        in_specs=[pl.BlockSpec((tb, d), lambda b, s: (b * nsb + s, 0)),
                  pl.BlockSpec((1, d), const2), vec, vec, vec,
                  pl.BlockSpec((d, 2 * c_rnn), const2),
                  pl.BlockSpec((nct, CONV_WIDTH, LANES), const3),
                  pl.BlockSpec((nct, 1, LANES), const3),
                  pl.BlockSpec((c_rnn, 2 * c_rnn), const2),
                  pl.BlockSpec((2 * nct, 1, LANES), const3),
                  pl.BlockSpec((nct, 1, LANES), const3),
                  pl.BlockSpec((c_rnn, d), const2)],
        out_specs=pl.BlockSpec((tb, d), lambda b, s: (b * nsb + s, 0)),
        out_shape=jax.ShapeDtypeStruct((n, d), F32),
        scratch_shapes=[pltpu.VMEM((2 * nct, tb, LANES), F32),
                        pltpu.VMEM((nct, tb, LANES), F32),
                        pltpu.VMEM((tb, c_rnn), BF16),
                        pltpu.VMEM((2 * nct, tb, LANES), F32),
                        pltpu.VMEM((tb, c_rnn), BF16),
                        pltpu.VMEM((nct, SUBLANES, LANES), F32),
                        pltpu.VMEM((nct, 1, LANES), F32)],
        compiler_params=_cparams(("arbitrary", "arbitrary")),
        name="rglru_mixer",
    )(x2, g.reshape(1, d), shift, scale, gate, in_w.astype(BF16), cw, cb, gw, gb, lam3, out_w.astype(BF16))


_UNRANKED = 99.0


def _top16_rows(cur, dst_ref, want_rank):
    rank = jnp.full(cur.shape, _UNRANKED, F32) if want_rank else None
    m = None
    for r in range(PEER_TOPK):
        m = jnp.max(cur, axis=0, keepdims=True)
        if dst_ref is not None:
            dst_ref[r:r + 1, :] = m
        hit = cur == m
        if want_rank:
            rank = jnp.where(hit, float(r), rank)
        if r + 1 < PEER_TOPK:
            cur = jnp.where(hit, _NEG_INF, cur)
    return rank, m


def _peer_select(s1, s2, v1_ref, v2_ref):
    rank1, _ = _top16_rows(s1, v1_ref, True)
    rank2, _ = _top16_rows(s2, v2_ref, True)
    row8 = lax.broadcasted_iota(jnp.int32, (SUBLANES, LANES), 0)
    blocks = [v1_ref[0:1, :] + v2_ref[...]]
    for i in range(1, SUBLANES):
        blk = v1_ref[i:i + 1, :] + v2_ref[0:SUBLANES, :]
        blocks.append(jnp.where(row8 < PEER_TOPK // (i + 1), blk, _NEG_INF))
    blocks.append(v1_ref[SUBLANES:PEER_TOPK, :] + v2_ref[0:1, :])
    cand = jnp.concatenate(blocks, axis=0)
    _, tau = _top16_rows(cand, None, False)
    m1 = v1_ref[0:1, :]
    m2 = v2_ref[0:1, :]
    z = jnp.sum(jnp.where(cand >= tau, jnp.exp(cand - (m1 + m2)), 0.0), axis=0, keepdims=True)
    cnt = jnp.zeros(s1.shape, F32)
    for i in range(SUBLANES):
        cnt_i = jnp.sum(jnp.where(blocks[i] >= tau, 1.0, 0.0), axis=0, keepdims=True)
        cnt = jnp.where(rank1 == float(i), cnt_i, cnt)
    cnt_hi = jnp.where(blocks[SUBLANES] >= tau, 1.0, 0.0)
    for r in range(SUBLANES):
        cnt = jnp.where(rank1 == float(SUBLANES + r), cnt_hi[r:r + 1, :], cnt)
    return cnt, jnp.exp(s1 - m1) * (1.0 / z), rank2, jnp.exp(s2 - m2)


def _peer_kernel(x_ref, g_ref, sh_ref, sc_ref, gt_ref, qwt_ref, k1_ref, k2_ref, u_ref, vt_ref, o_ref,
                 ht_ref, qt_ref, r2_ref, p2_ref, cnt_ref, p1_ref, v1_ref, v2_ref, acc_ref, bc_ref, bp_ref, at_ref, wt_ref,
                 *, tm, me):
    e = pl.program_id(1)
    n_e = pl.num_programs(1)
    nchunk = tm // LANES
    nslab = me // PEER_KEYS
    half = PEER_KEYS // 2

    @pl.when(e == 0)
    def _prologue():
        h = _norm_mod(x_ref[...], g_ref[...], sh_ref[0], sc_ref[0])
        ht = h.T.astype(BF16)
        ht_ref[...] = ht
        qt = jnp.dot(qwt_ref[...], ht, preferred_element_type=F32)
        for i in range(2 * PEER_HEADS):
            qt_ref[i] = qt[i * PEER_HALF:(i + 1) * PEER_HALF, :].astype(BF16)
        acc_ref[...] = jnp.zeros_like(acc_ref)

        def head_body(hh, carry):
            s1 = jnp.dot(k1_ref[hh], qt_ref[2 * hh], preferred_element_type=F32)
            s2 = jnp.dot(k2_ref[hh], qt_ref[2 * hh + 1], preferred_element_type=F32)
            for c in range(nchunk):
                lanes = slice(c * LANES, (c + 1) * LANES)
                cnt, p1, rank2, p2 = _peer_select(s1[:, lanes], s2[:, lanes], v1_ref.at[c], v2_ref.at[c])
                cnt_ref[hh, :, lanes] = cnt
                p1_ref[hh, :, lanes] = p1
                r2_ref[hh, :, lanes] = rank2.astype(GATE_DT)
                p2_ref[hh, :, lanes] = p2.astype(GATE_DT)
            return carry

        lax.fori_loop(0, PEER_HEADS, head_body, 0)

    e1_0 = pl.multiple_of(e * nslab, SUBLANES)
    for hh in range(PEER_HEADS):
        cnt8 = cnt_ref[hh, pl.ds(e1_0, nslab), :]
        p18 = p1_ref[hh, pl.ds(e1_0, nslab), :]
        for sl in range(nslab):
            bc_ref[sl, hh] = jnp.broadcast_to(cnt8[sl:sl + 1, :], (GATE_VROWS, tm)).astype(GATE_DT)
            bp_ref[sl, hh] = jnp.broadcast_to(p18[sl:sl + 1, :], (GATE_VROWS, tm)).astype(GATE_DT)

    at_ref[...] = jnp.dot(u_ref[...], ht_ref[...], preferred_element_type=F32)

    reps = GATE_ROWS // GATE_VROWS

    def rows_body(r, carry):
        k0 = pl.multiple_of(r * GATE_ROWS, GATE_ROWS)
        for c in range(nchunk):
            lanes = slice(c * LANES, (c + 1) * LANES)
            zero = jnp.zeros((GATE_ROWS, LANES), GATE_DT)
            gs = [zero for _ in range(nslab)]
            for hh in range(PEER_HEADS):
                rank2 = r2_ref[hh, pl.ds(k0, GATE_ROWS), lanes]
                p2 = p2_ref[hh, pl.ds(k0, GATE_ROWS), lanes]
                for sl in range(nslab):
                    cnt = pltpu.repeat(bc_ref[sl, hh, :, lanes], reps, axis=0)
                    p1 = pltpu.repeat(bp_ref[sl, hh, :, lanes], reps, axis=0)
                    gs[sl] = gs[sl] + jnp.where(rank2 < cnt, p2, zero) * p1
            for sl in range(nslab):
                rows = pl.ds(sl * PEER_KEYS + k0, GATE_ROWS)
                wt_ref[rows, lanes] = (gs[sl] * _gelu(at_ref[rows, lanes]).astype(GATE_DT)).astype(BF16)
        return carry

    lax.fori_loop(0, PEER_KEYS // GATE_ROWS, rows_body, 0)

    acc_ref[...] += jnp.dot(vt_ref[...], wt_ref[...], preferred_element_type=F32)

    @pl.when(e == n_e - 1)
    def _fin():
        o_ref[...] = x_ref[...] + gt_ref[0] * acc_ref[...].T


def _peer_layer(x2, n_batch, seq, g, shift, scale, gate, q_w, subkey1, subkey2, expert_u, expert_v):
    n, d = x2.shape
    n_exp = expert_u.shape[0]
    tm = min(512, seq)
    me = SUBLANES * PEER_KEYS
    nst = seq // tm
    qdim = q_w.shape[1]
    qwt = q_w.T.astype(BF16)
    u = expert_u.astype(BF16)
    vt = expert_v.T.astype(BF16)
    const2 = lambda i, e: (0, 0)
    const3 = lambda i, e: (0, 0, 0)
    vec = pl.BlockSpec((1, 1, d), lambda i, e: (i // nst, 0, 0))
    hk = (PEER_HEADS, PEER_KEYS, tm)
    return pl.pallas_call(
        functools.partial(_peer_kernel, tm=tm, me=me),
        grid=(n // tm, n_exp // me),
        in_specs=[pl.BlockSpec((tm, d), lambda i, e: (i, 0)),
                  pl.BlockSpec((1, d), const2), vec, vec, vec,
                  pl.BlockSpec((qdim, d), const2),
                  pl.BlockSpec((PEER_HEADS, PEER_KEYS, PEER_HALF), const3),
                  pl.BlockSpec((PEER_HEADS, PEER_KEYS, PEER_HALF), const3),
                  pl.BlockSpec((me, d), lambda i, e: (e, 0)),
                  pl.BlockSpec((d, me), lambda i, e: (0, e))],
        out_specs=pl.BlockSpec((tm, d), lambda i, e: (i, 0)),
        out_shape=jax.ShapeDtypeStruct((n, d), F32),
        scratch_shapes=[pltpu.VMEM((d, tm), BF16),
                        pltpu.VMEM((2 * PEER_HEADS, PEER_HALF, tm), BF16),
                        pltpu.VMEM(hk, GATE_DT),
                        pltpu.VMEM(hk, GATE_DT),
                        pltpu.VMEM(hk, F32),
                        pltpu.VMEM(hk, F32),
                        pltpu.VMEM((tm // LANES, PEER_TOPK, LANES), F32),
                        pltpu.VMEM((tm // LANES, PEER_TOPK, LANES), F32),
                        pltpu.VMEM((d, tm), F32),
                        pltpu.VMEM((me // PEER_KEYS, PEER_HEADS, GATE_VROWS, tm), GATE_DT),
                        pltpu.VMEM((me // PEER_KEYS, PEER_HEADS, GATE_VROWS, tm), GATE_DT),
                        pltpu.VMEM((me, tm), F32),
                        pltpu.VMEM((me, tm), BF16)],
        compiler_params=_cparams(("arbitrary", "arbitrary")),
        name="peer_dense",
    )(x2, g.reshape(1, d), shift, scale, gate, qwt, subkey1.astype(BF16), subkey2.astype(BF16), u, vt)


def _fox_proj_kernel(x_ref, g_ref, sh_ref, sc_ref, w_ref, wf_ref, fb_ref, qg_ref, kg_ref, bd_ref,
                     q_ref, k_ref, v_ref, og_ref, cumt_ref, cum_ref, carry_ref, *, tm, d):
    s = pl.program_id(1)

    @pl.when(s == 0)
    def _():
        carry_ref[...] = jnp.zeros_like(carry_ref)

    h = _norm_mod(x_ref[...], g_ref[...], sh_ref[0], sc_ref[0])
    hb = h.astype(BF16)
    y = jnp.dot(hb, w_ref[...], preferred_element_type=F32)
    bd = bd_ref[...]

    def head_norm(z, gain, post):
        z2 = z * z
        hi = z2.astype(BF16)
        lo = (z2 - hi.astype(F32)).astype(BF16)
        outs = []
        for t in range(d // LANES):
            sl = slice(t * LANES, (t + 1) * LANES)
            ms = (jnp.dot(hi[:, sl], bd, preferred_element_type=F32)
                  + jnp.dot(lo[:, sl], bd, preferred_element_type=F32))
            outs.append(z[:, sl] * lax.rsqrt(ms + NORM_EPS) * (gain[:, sl] * post))
        return jnp.concatenate(outs, axis=1)

    q_ref[0, 0] = head_norm(y[:, 0:d], qg_ref[...], FOX_HEAD_DIM ** -0.5).T.astype(BF16)
    k_ref[...] = head_norm(y[:, d:2 * d], kg_ref[...], 1.0).astype(BF16)
    vt = y[:, 2 * d:3 * d].T.astype(BF16)
    for j in range(tm // FOX_TK):
        v_ref[0, j] = vt[:, j * FOX_TK:(j + 1) * FOX_TK]
    og_ref[...] = jax.nn.sigmoid(y[:, 3 * d:4 * d])

    lf = jnp.dot(hb, wf_ref[...], preferred_element_type=F32) + fb_ref[...]
    c = -_softplus(-lf)
    row = lax.broadcasted_iota(jnp.int32, (tm, LANES), 0)
    k = 1
    while k < tm:
        c = c + jnp.where(row >= k, pltpu.roll(c, k, axis=0), 0.0)
        k *= 2
    cum = c + carry_ref[...]
    carry_ref[...] = cum[tm - 1:tm, :]
    cum_ref[...] = cum
    cumt_ref[0] = cum.T[0:FOX_HEADS, :]


def _fox_attn_kernel(qt_ref, k_ref, vt_ref, cum_ref, cumt_ref, o_ref, ckb_ref, s_ref, *, tq, tk, seq):
    hp = pl.program_id(1)
    qi = pl.program_id(2)
    hd = FOX_HEAD_DIM
    nrep = tq // LANES
    per_q = tq // tk

    @pl.when(qi == 0)
    def _():
        lane = lax.broadcasted_iota(jnp.int32, (FOX_CK_ROWS, LANES), 1)
        for hh in range(2):
            for r in range(seq // FOX_CK_ROWS):
                rows = slice(r * FOX_CK_ROWS, (r + 1) * FOX_CK_ROWS)
                col = jnp.sum(jnp.where(lane == 2 * hp + hh, cum_ref[rows, :], 0.0), axis=1, keepdims=True)
                ckb_ref[hh, rows, :] = jnp.broadcast_to(col, (FOX_CK_ROWS, LANES))

    qt = qt_ref[0, 0]
    cq = [cumt_ref[0, 2 * hp + hh, pl.ds(qi, 1), :] for hh in range(2)]
    lane_k = lax.broadcasted_iota(jnp.int32, (tk, LANES), 1)
    row_v = lax.broadcasted_iota(jnp.int32, (LANES, tk), 0)
    row_a = lax.broadcasted_iota(jnp.int32, (LANES, tq), 0)
    rowi = lax.broadcasted_iota(jnp.int32, (tk, tq), 0)
    coli = lax.broadcasted_iota(jnp.int32, (tk, tq), 1)

    def score_chunk(j, mx, diag_off):
        off = pl.multiple_of(j * tk, tk)
        kk = k_ref[pl.ds(off, tk), :]
        zk = jnp.zeros_like(kk)
        kbd = jnp.concatenate([jnp.where(lane_k < hd, kk, zk), jnp.where(lane_k >= hd, kk, zk)], axis=0)
        st = jnp.dot(kbd, qt, preferred_element_type=F32)
        out = []
        for hh in range(2):
            ck = ckb_ref[hh, pl.ds(off, tk), :]
            s = st[hh * tk:(hh + 1) * tk, :] + (cq[hh] - jnp.concatenate([ck] * nrep, axis=1))
            if diag_off is not None:
                s = jnp.where(coli >= rowi + diag_off, s, _NEG_INF)
            s_ref[j, hh * tk:(hh + 1) * tk, :] = s
            out.append(jnp.maximum(mx[hh], jnp.max(s, axis=0, keepdims=True)))
        return tuple(out)

    def blocks(fn, nblk):
        def body(jb, carry):
            for sub in range(nblk * per_q):
                carry = fn(jb * nblk * per_q + sub, carry)
            return carry
        return body

    def run(fn, n, init):
        carry = lax.fori_loop(0, n // FOX_UNROLL, blocks(fn, FOX_UNROLL), init)
        return lax.fori_loop((n // FOX_UNROLL) * FOX_UNROLL, n, blocks(fn, 1), carry)

    neg = jnp.full((1, tq), _NEG_INF, F32)
    mx = run(lambda j, c: score_chunk(j, c, None), qi, (neg, neg))
    for dgl in range(per_q):
        mx = score_chunk(qi * per_q + dgl, mx, dgl * tk)

    def value_chunk(j, carry):
        l0, l1, acc = carry
        p0 = jnp.exp(s_ref[j, 0:tk, :] - mx[0])
        p1 = jnp.exp(s_ref[j, tk:2 * tk, :] - mx[1])
        vt = vt_ref[0, j]
        zv = jnp.zeros_like(vt)
        vbd = jnp.concatenate([jnp.where(row_v < hd, vt, zv), jnp.where(row_v >= hd, vt, zv)], axis=1)
        pt = jnp.concatenate([p0.astype(BF16), p1.astype(BF16)], axis=0)
        acc = acc + jnp.dot(vbd, pt, preferred_element_type=F32)
        return (l0 + jnp.sum(p0, axis=0, keepdims=True), l1 + jnp.sum(p1, axis=0, keepdims=True), acc)

    zero = jnp.zeros((1, tq), F32)
    l0, l1, acc = run(value_chunk, qi + 1, (zero, zero, jnp.zeros((LANES, tq), F32)))
    o_ref[...] = (acc * (1.0 / jnp.where(row_a < hd, l0, l1))).T


def _fox_out_kernel(x_ref, o_ref_in, og_ref, gt_ref, w_ref, out_ref):
    y = (o_ref_in[...] * og_ref[...]).astype(BF16)
    out_ref[...] = x_ref[...] + gt_ref[0] * jnp.dot(y, w_ref[...], preferred_element_type=F32)


def _fox_layer(x2, n_batch, seq, g, shift, scale, gate, in_w, f_b, qg, kg, out_w):
    n, d = x2.shape
    nh = FOX_HEADS
    tm = 256
    tk = FOX_TK
    nst = seq // tm
    w_main = jnp.concatenate([in_w[:, :3 * d], in_w[:, 3 * d + nh:]], axis=1).astype(BF16)
    wf = jnp.pad(in_w[:, 3 * d:3 * d + nh], ((0, 0), (0, LANES - nh))).astype(BF16)
    fb = jnp.pad(f_b, (0, LANES - nh)).reshape(1, LANES)
    qg_t = jnp.tile(qg, nh).reshape(1, d)
    kg_t = jnp.tile(kg, nh).reshape(1, d)
    ii = jnp.arange(LANES) // FOX_HEAD_DIM
    bd = jnp.where(ii[:, None] == ii[None, :], 1.0 / FOX_HEAD_DIM, 0.0).astype(BF16)
    const2 = lambda b, s: (0, 0)
    vec = pl.BlockSpec((1, 1, d), lambda b, s: (b, 0, 0))
    tok = pl.BlockSpec((tm, d), lambda b, s: (b * nst + s, 0))
    q, k, v, og, cumt, cum = pl.pallas_call(
        functools.partial(_fox_proj_kernel, tm=tm, d=d),
        grid=(n_batch, nst),
        in_specs=[tok, pl.BlockSpec((1, d), const2), vec, vec,
                  pl.BlockSpec((d, 4 * d), const2),
                  pl.BlockSpec((d, LANES), const2),
                  pl.BlockSpec((1, LANES), const2),
                  pl.BlockSpec((1, d), const2),
                  pl.BlockSpec((1, d), const2),
                  pl.BlockSpec((LANES, LANES), const2)],
        out_specs=[pl.BlockSpec((1, 1, d, tm), lambda b, s: (b, s, 0, 0)), tok,
                   pl.BlockSpec((1, tm // tk, d, tk), lambda b, s: (b, s, 0, 0)), tok,
                   pl.BlockSpec((1, nh, tm), lambda b, s: (b, 0, s)),
                   pl.BlockSpec((tm, LANES), lambda b, s: (b * nst + s, 0))],
        out_shape=[jax.ShapeDtypeStruct((n_batch, nst, d, tm), BF16), jax.ShapeDtypeStruct((n, d), BF16),
                   jax.ShapeDtypeStruct((n_batch, seq // tk, d, tk), BF16), jax.ShapeDtypeStruct((n, d), F32),
                   jax.ShapeDtypeStruct((n_batch, nh, seq), F32),
                   jax.ShapeDtypeStruct((n, LANES), F32)],
        scratch_shapes=[pltpu.VMEM((1, LANES), F32)],
        compiler_params=_cparams(("arbitrary", "arbitrary")),
        name="fox_proj",
    )(x2, g.reshape(1, d), shift, scale, w_main, wf, fb, qg_t, kg_t, bd)

    tq = tm
    nq = seq // tq
    cumt4 = cumt.reshape(n_batch, nh, nq, tq)
    o = pl.pallas_call(
        functools.partial(_fox_attn_kernel, tq=tq, tk=tk, seq=seq),
        grid=(n_batch, nh // 2, nq),
        in_specs=[pl.BlockSpec((1, 1, LANES, tq), lambda b, hp, qi: (b, qi, hp, 0)),
                  pl.BlockSpec((seq, LANES), lambda b, hp, qi: (b, hp)),
                  pl.BlockSpec((1, seq // tk, LANES, tk), lambda b, hp, qi: (b, 0, hp, 0)),
                  pl.BlockSpec((seq, LANES), lambda b, hp, qi: (b, 0)),
                  pl.BlockSpec((1, nh, nq, tq), lambda b, hp, qi: (b, 0, 0, 0))],
        out_specs=pl.BlockSpec((tq, LANES), lambda b, hp, qi: (b * nq + qi, hp)),
        out_shape=jax.ShapeDtypeStruct((n, d), F32),
        scratch_shapes=[pltpu.VMEM((2, seq, LANES), F32),
                        pltpu.VMEM((seq // tk, 2 * tk, tq), F32)],
        compiler_params=_cparams(("arbitrary", "arbitrary", "arbitrary")),
        name="fox_attn",
    )(q, k, v, cum, cumt4)

    to = 512
    tok2 = pl.BlockSpec((to, d), lambda i: (i, 0))
    return pl.pallas_call(
        _fox_out_kernel,
        grid=(n // to,),
        in_specs=[tok2, tok2, tok2,
                  pl.BlockSpec((1, 1, d), lambda i: (i // (seq // to), 0, 0)),
                  pl.BlockSpec((d, d), lambda i: (0, 0))],
        out_specs=tok2,
        out_shape=jax.ShapeDtypeStruct((n, d), F32),
        compiler_params=_cparams(("arbitrary",)),
        name="fox_out",
    )(x2, o, og, gate, out_w.astype(BF16))


def kernel(x, c, l0_mix_norm_g, l0_mix_mod_w, l0_mix_mod_b, l0_lru_in_w, l0_lru_conv_w, l0_lru_conv_b, l0_lru_ra_w, l0_lru_ra_b, l0_lru_ri_w, l0_lru_ri_b, l0_lru_lambda, l0_lru_out_w, l0_ffn_norm_g, l0_ffn_mod_w, l0_ffn_mod_b, l0_peer_q_w, l0_peer_subkey1, l0_peer_subkey2, l0_peer_u, l0_peer_v, l1_mix_norm_g, l1_mix_mod_w, l1_mix_mod_b, l1_fox_in_w, l1_fox_f_b, l1_fox_q_norm_g, l1_fox_k_norm_g, l1_fox_out_w, l1_ffn_norm_g, l1_ffn_mod_w, l1_ffn_mod_b, l1_peer_q_w, l1_peer_subkey1, l1_peer_subkey2, l1_peer_u, l1_peer_v):
    n_batch, seq, d = x.shape
    x2 = x.reshape(n_batch * seq, d)
    c_pad = jnp.pad(c, ((0, SUBLANES - n_batch % SUBLANES), (0, 0))) if n_batch % SUBLANES else c

    sh, sc, gt = _ada_mod(c_pad, l0_mix_mod_w, l0_mix_mod_b, n_batch)
    x2 = _lru_layer(x2, n_batch, seq, l0_mix_norm_g, sh, sc, gt, l0_lru_in_w, l0_lru_conv_w, l0_lru_conv_b,
                    l0_lru_ra_w, l0_lru_ra_b, l0_lru_ri_w, l0_lru_ri_b, l0_lru_lambda, l0_lru_out_w)
    sh, sc, gt = _ada_mod(c_pad, l0_ffn_mod_w, l0_ffn_mod_b, n_batch)
    x2 = _peer_layer(x2, n_batch, seq, l0_ffn_norm_g, sh, sc, gt, l0_peer_q_w, l0_peer_subkey1, l0_peer_subkey2,
                     l0_peer_u, l0_peer_v)
    sh, sc, gt = _ada_mod(c_pad, l1_mix_mod_w, l1_mix_mod_b, n_batch)
    x2 = _fox_layer(x2, n_batch, seq, l1_mix_norm_g, sh, sc, gt, l1_fox_in_w, l1_fox_f_b, l1_fox_q_norm_g,
                    l1_fox_k_norm_g, l1_fox_out_w)
    sh, sc, gt = _ada_mod(c_pad, l1_ffn_mod_w, l1_ffn_mod_b, n_batch)
    x2 = _peer_layer(x2, n_batch, seq, l1_ffn_norm_g, sh, sc, gt, l1_peer_q_w, l1_peer_subkey1, l1_peer_subkey2,
                     l1_peer_u, l1_peer_v)
    return x2.reshape(n_batch, seq, d)
```

```python
import functools

import jax
import jax.numpy as jnp
from jax import lax
from jax.experimental import pallas as pl
from jax.experimental.pallas import tpu as pltpu

F32 = jnp.float32
BF16 = jnp.bfloat16

NORM_EPS = 1e-6
LRU_BLOCKS = 16
LRU_C = 8.0
CONV_WIDTH = 4
FOX_HEADS = 16
FOX_HEAD_DIM = 64
FOX_TK = 128
FOX_CK_ROWS = 512
FOX_UNROLL = 4
PEER_HEADS = 8
PEER_KEYS = 128
PEER_TOPK = 16
PEER_HALF = 128

LANES = 128
SUBLANES = 8
GATE_DT = jnp.float32
GATE_VROWS = 8
GATE_ROWS = 32
GATE_SLABS = 4
VMEM_LIMIT = 56 * 1024 * 1024

_SQRT_HALF = 0.7071067811865476
_NEG_INF = float("-inf")


def _gelu(x):
    return 0.5 * x * (1.0 + lax.erf(x * _SQRT_HALF))


def _softplus(z):
    return jnp.maximum(z, 0.0) + jnp.log1p(jnp.exp(-jnp.abs(z)))


def _norm_mod(x, g, shift, scale):
    ms = jnp.mean(x * x, axis=-1, keepdims=True)
    y = x * lax.rsqrt(ms + NORM_EPS) * g
    return y * (1.0 + scale) + shift


def _cparams(sem, flags=None):
    return pltpu.CompilerParams(dimension_semantics=sem, vmem_limit_bytes=VMEM_LIMIT, flags=flags)


def _mod_kernel(c_ref, w_ref, b_ref, o_ref):
    c = c_ref[...]
    s = c * jax.nn.sigmoid(c)
    o_ref[...] = jnp.dot(s.astype(BF16), w_ref[...].astype(BF16), preferred_element_type=F32) + b_ref[...]


def _ada_mod(c_pad, w, b, n_batch):
    d, d3 = w.shape
    tn = 512
    mod = pl.pallas_call(
        _mod_kernel,
        grid=(d3 // tn,),
        in_specs=[pl.BlockSpec(c_pad.shape, lambda j: (0, 0)),
                  pl.BlockSpec((d, tn), lambda j: (0, j)),
                  pl.BlockSpec((1, tn), lambda j: (0, j))],
        out_specs=pl.BlockSpec((c_pad.shape[0], tn), lambda j: (0, j)),
        out_shape=jax.ShapeDtypeStruct((c_pad.shape[0], d3), F32),
        compiler_params=_cparams(("arbitrary",)),
        name="ada_mod",
    )(c_pad, w, b.reshape(1, d3))
    mod = mod[:n_batch]
    shift, scale, gate = jnp.split(mod, 3, axis=-1)
    return shift[:, None, :], scale[:, None, :], gate[:, None, :]


def _lru_kernel(x_ref, g_ref, sh_ref, sc_ref, gt_ref, inw_ref, cw_ref, cb_ref, gw_ref, gb_ref, lam_ref, ow_ref,
                o_ref, y3_ref, xc3_ref, xcb_ref, gz3_ref, yob_ref, xprev_ref, hprev_ref, *, tb, nct):
    s = pl.program_id(1)

    @pl.when(s == 0)
    def _():
        xprev_ref[...] = jnp.zeros_like(xprev_ref)
        hprev_ref[...] = jnp.zeros_like(hprev_ref)

    x = x_ref[...]
    h = _norm_mod(x, g_ref[...], sh_ref[0], sc_ref[0])
    y = jnp.dot(h.astype(BF16), inw_ref[...], preferred_element_type=F32)
    for c in range(2 * nct):
        y3_ref[c] = y[:, c * LANES:(c + 1) * LANES]

    for c in range(nct):
        xb = y3_ref[c]
        full = jnp.concatenate([xprev_ref[c], xb], axis=0)
        xprev_ref[c] = xb[tb - SUBLANES:, :]
        cw = cw_ref[c]
        xc = cb_ref[c] + cw[CONV_WIDTH - 1:CONV_WIDTH] * xb
        for j in range(1, CONV_WIDTH):
            shifted = pltpu.roll(full, j, axis=0)[SUBLANES:, :]
            xc = xc + cw[CONV_WIDTH - 1 - j:CONV_WIDTH - j] * shifted
        xc3_ref[c] = xc
        xcb_ref[:, c * LANES:(c + 1) * LANES] = xc.astype(BF16)

    gz = jnp.dot(xcb_ref[...], gw_ref[...], preferred_element_type=F32)
    for c in range(2 * nct):
        gz3_ref[c] = gz[:, c * LANES:(c + 1) * LANES]

    row = lax.broadcasted_iota(jnp.int32, (tb, LANES), 0)

    def tile_body(c, carry):
        xc = xc3_ref[c]
        r = jax.nn.sigmoid(gz3_ref[c] + gb_ref[c])
        i = jax.nn.sigmoid(gz3_ref[nct + c] + gb_ref[nct + c])
        log_a = (-LRU_C * _softplus(-lam_ref[c])) * r
        a = jnp.exp(log_a)
        u = jnp.sqrt(1.0 - jnp.exp(2.0 * log_a)) * (i * xc)
        k = 1
        while k < tb:
            keep = row >= k
            a_sh = jnp.where(keep, pltpu.roll(a, k, axis=0), 1.0)
            u_sh = jnp.where(keep, pltpu.roll(u, k, axis=0), 0.0)
            u = a * u_sh + u
            a = a * a_sh
            k *= 2
        hs = u + a * hprev_ref[c]
        hprev_ref[c] = hs[tb - 1:tb, :]
        y3_ref[c] = hs * _gelu(y3_ref[nct + c])
        return carry

    lax.fori_loop(0, nct, tile_body, 0)

    for c in range(nct):
        yob_ref[:, c * LANES:(c + 1) * LANES] = y3_ref[c].astype(BF16)
    out = jnp.dot(yob_ref[...], ow_ref[...], preferred_element_type=F32)
    o_ref[...] = x + gt_ref[0] * out


def _lru_layer(x2, n_batch, seq, g, shift, scale, gate, in_w, conv_w, conv_b, ra_w, ra_b, ri_w, ri_b, lam, out_w):
    n, d = x2.shape
    c_rnn = conv_w.shape[1]
    nct = c_rnn // LANES
    tb = 256
    nsb = seq // tb
    eye = jnp.eye(LRU_BLOCKS, dtype=F32)
    ra_d = jnp.einsum('nm,ncd->ncmd', eye, ra_w).reshape(c_rnn, c_rnn)
    ri_d = jnp.einsum('nm,ncd->ncmd', eye, ri_w).reshape(c_rnn, c_rnn)
    gw = jnp.concatenate([ra_d, ri_d], axis=1).astype(BF16)
    gb = jnp.concatenate([ra_b, ri_b]).reshape(2 * nct, 1, LANES)
    cw = conv_w.reshape(CONV_WIDTH, nct, LANES).transpose(1, 0, 2)
    cb = conv_b.reshape(nct, 1, LANES)
    lam3 = lam.reshape(nct, 1, LANES)
    const2 = lambda b, s: (0, 0)
    const3 = lambda b, s: (0, 0, 0)
    vec = pl.BlockSpec((1, 1, d), lambda b, s: (b, 0, 0))
    return pl.pallas_call(
        functools.partial(_lru_kernel, tb=tb, nct=nct),
        grid=(n_batch, nsb),
        in_specs=[pl.BlockSpec((tb, d), lambda b, s: (b * nsb + s, 0)),
                  pl.BlockSpec((1, d), const2), vec, vec, vec,
                  pl.BlockSpec((d, 2 * c_rnn), const2),
                  pl.BlockSpec((nct, CONV_WIDTH, LANES), const3),
                  pl.BlockSpec((nct, 1, LANES), const3),
                  pl.BlockSpec((c_rnn, 2 * c_rnn), const2),
                  pl.BlockSpec((2 * nct, 1, LANES), const3),
                  pl.BlockSpec((nct, 1, LANES), const3),
                  pl.BlockSpec((c_rnn, d), const2)],
        out_specs=pl.BlockSpec((tb, d), lambda b, s: (b * nsb + s, 0)),
        out_shape=jax.ShapeDtypeStruct((n, d), F32),
        scratch_shapes=[pltpu.VMEM((2 * nct, tb, LANES), F32),
                        pltpu.VMEM((nct, tb, LANES), F32),
                        pltpu.VMEM((tb, c_rnn), BF16),
                        pltpu.VMEM((2 * nct, tb, LANES), F32),
                        pltpu.VMEM((tb, c_rnn), BF16),
                        pltpu.VMEM((nct, SUBLANES, LANES), F32),
                        pltpu.VMEM((nct, 1, LANES), F32)],
        compiler_params=_cparams(("arbitrary", "arbitrary")),
        name="rglru_mixer",
    )(x2, g.reshape(1, d), shift, scale, gate, in_w.astype(BF16), cw, cb, gw, gb, lam3, out_w.astype(BF16))


_UNRANKED = 99.0


def _top16_rows(cur, dst_ref, want_rank):
    rank = jnp.full(cur.shape, _UNRANKED, F32) if want_rank else None
    m = None
    for r in range(PEER_TOPK):
        m = jnp.max(cur, axis=0, keepdims=True)
        if dst_ref is not None:
            dst_ref[r:r + 1, :] = m
        hit = cur == m
        if want_rank:
            rank = jnp.where(hit, float(r), rank)
        if r + 1 < PEER_TOPK:
            cur = jnp.where(hit, _NEG_INF, cur)
    return rank, m


def _peer_select(s1, s2, v1_ref, v2_ref):
    rank1, _ = _top16_rows(s1, v1_ref, True)
    rank2, _ = _top16_rows(s2, v2_ref, True)
    row8 = lax.broadcasted_iota(jnp.int32, (SUBLANES, LANES), 0)
    blocks = [v1_ref[0:1, :] + v2_ref[...]]
    for i in range(1, SUBLANES):
        blk = v1_ref[i:i + 1, :] + v2_ref[0:SUBLANES, :]
        blocks.append(jnp.where(row8 < PEER_TOPK // (i + 1), blk, _NEG_INF))
    blocks.append(v1_ref[SUBLANES:PEER_TOPK, :] + v2_ref[0:1, :])
    cand = jnp.concatenate(blocks, axis=0)
    _, tau = _top16_rows(cand, None, False)
    m1 = v1_ref[0:1, :]
    m2 = v2_ref[0:1, :]
    z = jnp.sum(jnp.where(cand >= tau, jnp.exp(cand - (m1 + m2)), 0.0), axis=0, keepdims=True)
    cnt = jnp.zeros(s1.shape, F32)
    for i in range(SUBLANES):
        cnt_i = jnp.sum(jnp.where(blocks[i] >= tau, 1.0, 0.0), axis=0, keepdims=True)
        cnt = jnp.where(rank1 == float(i), cnt_i, cnt)
    cnt_hi = jnp.where(blocks[SUBLANES] >= tau, 1.0, 0.0)
    for r in range(SUBLANES):
        cnt = jnp.where(rank1 == float(SUBLANES + r), cnt_hi[r:r + 1, :], cnt)
    return cnt, jnp.exp(s1 - m1) * (1.0 / z), rank2, jnp.exp(s2 - m2)


def _peer_kernel(x_ref, g_ref, sh_ref, sc_ref, gt_ref, qwt_ref, k1_ref, k2_ref, u_ref, vt_ref, o_ref,
                 ht_ref, qt_ref, r2_ref, p2_ref, cnt_ref, p1_ref, v1_ref, v2_ref, acc_ref, bc_ref, bp_ref, at_ref, wt_ref,
                 *, tm, me):
    e = pl.program_id(1)
    n_e = pl.num_programs(1)
    nchunk = tm // LANES
    nslab = me // PEER_KEYS
    half = PEER_KEYS // 2

    @pl.when(e == 0)
    def _prologue():
        h = _norm_mod(x_ref[...], g_ref[...], sh_ref[0], sc_ref[0])
        ht = h.T.astype(BF16)
        ht_ref[...] = ht
        qt = jnp.dot(qwt_ref[...], ht, preferred_element_type=F32)
        for i in range(2 * PEER_HEADS):
            qt_ref[i] = qt[i * PEER_HALF:(i + 1) * PEER_HALF, :].astype(BF16)
        acc_ref[...] = jnp.zeros_like(acc_ref)

        def head_body(hh, carry):
            s1 = jnp.dot(k1_ref[hh], qt_ref[2 * hh], preferred_element_type=F32)
            s2 = jnp.dot(k2_ref[hh], qt_ref[2 * hh + 1], preferred_element_type=F32)
            for c in range(nchunk):
                lanes = slice(c * LANES, (c + 1) * LANES)
                cnt, p1, rank2, p2 = _peer_select(s1[:, lanes], s2[:, lanes], v1_ref.at[c], v2_ref.at[c])
                cnt_ref[hh, :, lanes] = cnt
                p1_ref[hh, :, lanes] = p1
                r2_ref[hh, :, lanes] = rank2.astype(GATE_DT)
                p2_ref[hh, :, lanes] = p2.astype(GATE_DT)
            return carry

        lax.fori_loop(0, PEER_HEADS, head_body, 0)

    e1_0 = pl.multiple_of(e * nslab, SUBLANES)
    for hh in range(PEER_HEADS):
        cnt8 = cnt_ref[hh, pl.ds(e1_0, nslab), :]
        p18 = p1_ref[hh, pl.ds(e1_0, nslab), :]
        for sl in range(nslab):
            bc_ref[sl, hh] = jnp.broadcast_to(cnt8[sl:sl + 1, :], (GATE_VROWS, tm)).astype(GATE_DT)
            bp_ref[sl, hh] = jnp.broadcast_to(p18[sl:sl + 1, :], (GATE_VROWS, tm)).astype(GATE_DT)

    at_ref[...] = jnp.dot(u_ref[...], ht_ref[...], preferred_element_type=F32)

    reps = GATE_ROWS // GATE_VROWS

    def rows_body(r, carry):
        k0 = pl.multiple_of(r * GATE_ROWS, GATE_ROWS)
        for c in range(nchunk):
            lanes = slice(c * LANES, (c + 1) * LANES)
            zero = jnp.zeros((GATE_ROWS, LANES), GATE_DT)
            for s0 in range(0, nslab, GATE_SLABS):
                gs = [zero for _ in range(GATE_SLABS)]
                for hh in range(PEER_HEADS):
                    rank2 = r2_ref[hh, pl.ds(k0, GATE_ROWS), lanes]
                    p2 = p2_ref[hh, pl.ds(k0, GATE_ROWS), lanes]
                    for i in range(GATE_SLABS):
                        cnt = pltpu.repeat(bc_ref[s0 + i, hh, :, lanes], reps, axis=0)
                        p1 = pltpu.repeat(bp_ref[s0 + i, hh, :, lanes], reps, axis=0)
                        gs[i] = gs[i] + jnp.where(rank2 < cnt, p2, zero) * p1
                for i in range(GATE_SLABS):
                    rows = pl.ds((s0 + i) * PEER_KEYS + k0, GATE_ROWS)
                    wt_ref[rows, lanes] = (gs[i] * _gelu(at_ref[rows, lanes]).astype(GATE_DT)).astype(BF16)
        return carry

    lax.fori_loop(0, PEER_KEYS // GATE_ROWS, rows_body, 0)

    acc_ref[...] += jnp.dot(vt_ref[...], wt_ref[...], preferred_element_type=F32)

    @pl.when(e == n_e - 1)
    def _fin():
        o_ref[...] = x_ref[...] + gt_ref[0] * acc_ref[...].T


def _peer_layer(x2, n_batch, seq, g, shift, scale, gate, q_w, subkey1, subkey2, expert_u, expert_v):
    n, d = x2.shape
    n_exp = expert_u.shape[0]
    tm = min(512, seq)
    me = SUBLANES * PEER_KEYS
    nst = seq // tm
    qdim = q_w.shape[1]
    qwt = q_w.T.astype(BF16)
    u = expert_u.astype(BF16)
    vt = expert_v.T.astype(BF16)
    const2 = lambda i, e: (0, 0)
    const3 = lambda i, e: (0, 0, 0)
    vec = pl.BlockSpec((1, 1, d), lambda i, e: (i // nst, 0, 0))
    hk = (PEER_HEADS, PEER_KEYS, tm)
    return pl.pallas_call(
        functools.partial(_peer_kernel, tm=tm, me=me),
        grid=(n // tm, n_exp // me),
        in_specs=[pl.BlockSpec((tm, d), lambda i, e: (i, 0)),
                  pl.BlockSpec((1, d), const2), vec, vec, vec,
                  pl.BlockSpec((qdim, d), const2),
                  pl.BlockSpec((PEER_HEADS, PEER_KEYS, PEER_HALF), const3),
                  pl.BlockSpec((PEER_HEADS, PEER_KEYS, PEER_HALF), const3),
                  pl.BlockSpec((me, d), lambda i, e: (e, 0)),
                  pl.BlockSpec((d, me), lambda i, e: (0, e))],
        out_specs=pl.BlockSpec((tm, d), lambda i, e: (i, 0)),
        out_shape=jax.ShapeDtypeStruct((n, d), F32),
        scratch_shapes=[pltpu.VMEM((d, tm), BF16),
                        pltpu.VMEM((2 * PEER_HEADS, PEER_HALF, tm), BF16),
                        pltpu.VMEM(hk, GATE_DT),
                        pltpu.VMEM(hk, GATE_DT),
                        pltpu.VMEM(hk, F32),
                        pltpu.VMEM(hk, F32),
                        pltpu.VMEM((tm // LANES, PEER_TOPK, LANES), F32),
                        pltpu.VMEM((tm // LANES, PEER_TOPK, LANES), F32),
                        pltpu.VMEM((d, tm), F32),
                        pltpu.VMEM((me // PEER_KEYS, PEER_HEADS, GATE_VROWS, tm), GATE_DT),
                        pltpu.VMEM((me // PEER_KEYS, PEER_HEADS, GATE_VROWS, tm), GATE_DT),
                        pltpu.VMEM((me, tm), F32),
                        pltpu.VMEM((me, tm), BF16)],
        compiler_params=_cparams(("arbitrary", "arbitrary")),
        name="peer_dense",
    )(x2, g.reshape(1, d), shift, scale, gate, qwt, subkey1.astype(BF16), subkey2.astype(BF16), u, vt)


def _fox_proj_kernel(x_ref, g_ref, sh_ref, sc_ref, w_ref, wf_ref, fb_ref, qg_ref, kg_ref, bd_ref,
                     q_ref, k_ref, v_ref, og_ref, cumt_ref, cum_ref, carry_ref, *, tm, d):
    s = pl.program_id(1)

    @pl.when(s == 0)
    def _():
        carry_ref[...] = jnp.zeros_like(carry_ref)

    h = _norm_mod(x_ref[...], g_ref[...], sh_ref[0], sc_ref[0])
    hb = h.astype(BF16)
    y = jnp.dot(hb, w_ref[...], preferred_element_type=F32)
    bd = bd_ref[...]

    def head_norm(z, gain, post):
        z2 = z * z
        hi = z2.astype(BF16)
        lo = (z2 - hi.astype(F32)).astype(BF16)
        outs = []
        for t in range(d // LANES):
            sl = slice(t * LANES, (t + 1) * LANES)
            ms = (jnp.dot(hi[:, sl], bd, preferred_element_type=F32)
                  + jnp.dot(lo[:, sl], bd, preferred_element_type=F32))
            outs.append(z[:, sl] * lax.rsqrt(ms + NORM_EPS) * (gain[:, sl] * post))
        return jnp.concatenate(outs, axis=1)

    q_ref[0, 0] = head_norm(y[:, 0:d], qg_ref[...], FOX_HEAD_DIM ** -0.5).T.astype(BF16)
    k_ref[...] = head_norm(y[:, d:2 * d], kg_ref[...], 1.0).astype(BF16)
    vt = y[:, 2 * d:3 * d].T.astype(BF16)
    for j in range(tm // FOX_TK):
        v_ref[0, j] = vt[:, j * FOX_TK:(j + 1) * FOX_TK]
    og_ref[...] = jax.nn.sigmoid(y[:, 3 * d:4 * d])

    lf = jnp.dot(hb, wf_ref[...], preferred_element_type=F32) + fb_ref[...]
    c = -_softplus(-lf)
    row = lax.broadcasted_iota(jnp.int32, (tm, LANES), 0)
    k = 1
    while k < tm:
        c = c + jnp.where(row >= k, pltpu.roll(c, k, axis=0), 0.0)
        k *= 2
    cum = c + carry_ref[...]
    carry_ref[...] = cum[tm - 1:tm, :]
    cum_ref[...] = cum
    cumt_ref[0] = cum.T[0:FOX_HEADS, :]


def _fox_attn_kernel(qt_ref, k_ref, vt_ref, cum_ref, cumt_ref, o_ref, ckb_ref, s_ref, *, tq, tk, seq):
    hp = pl.program_id(1)
    qi = pl.program_id(2)
    hd = FOX_HEAD_DIM
    nrep = tq // LANES
    per_q = tq // tk

    @pl.when(qi == 0)
    def _():
        lane = lax.broadcasted_iota(jnp.int32, (FOX_CK_ROWS, LANES), 1)
        for hh in range(2):
            for r in range(seq // FOX_CK_ROWS):
                rows = slice(r * FOX_CK_ROWS, (r + 1) * FOX_CK_ROWS)
                col = jnp.sum(jnp.where(lane == 2 * hp + hh, cum_ref[rows, :], 0.0), axis=1, keepdims=True)
                ckb_ref[hh, rows, :] = jnp.broadcast_to(col, (FOX_CK_ROWS, LANES))

    qt = qt_ref[0, 0]
    cq = [cumt_ref[0, 2 * hp + hh, pl.ds(qi, 1), :] for hh in range(2)]
    lane_k = lax.broadcasted_iota(jnp.int32, (tk, LANES), 1)
    row_v = lax.broadcasted_iota(jnp.int32, (LANES, tk), 0)
    row_a = lax.broadcasted_iota(jnp.int32, (LANES, tq), 0)
    rowi = lax.broadcasted_iota(jnp.int32, (tk, tq), 0)
    coli = lax.broadcasted_iota(jnp.int32, (tk, tq), 1)

    def score_chunk(j, mx, diag_off):
        off = pl.multiple_of(j * tk, tk)
        kk = k_ref[pl.ds(off, tk), :]
        zk = jnp.zeros_like(kk)
        kbd = jnp.concatenate([jnp.where(lane_k < hd, kk, zk), jnp.where(lane_k >= hd, kk, zk)], axis=0)
        st = jnp.dot(kbd, qt, preferred_element_type=F32)
        out = []
        for hh in range(2):
            ck = ckb_ref[hh, pl.ds(off, tk), :]
            s = st[hh * tk:(hh + 1) * tk, :] + (cq[hh] - jnp.concatenate([ck] * nrep, axis=1))
            if diag_off is not None:
                s = jnp.where(coli >= rowi + diag_off, s, _NEG_INF)
            s_ref[j, hh * tk:(hh + 1) * tk, :] = s
            out.append(jnp.maximum(mx[hh], jnp.max(s, axis=0, keepdims=True)))
        return tuple(out)

    def blocks(fn, nblk):
        def body(jb, carry):
            for sub in range(nblk * per_q):
                carry = fn(jb * nblk * per_q + sub, carry)
            return carry
        return body

    def run(fn, n, init):
        carry = lax.fori_loop(0, n // FOX_UNROLL, blocks(fn, FOX_UNROLL), init)
        return lax.fori_loop((n // FOX_UNROLL) * FOX_UNROLL, n, blocks(fn, 1), carry)

    neg = jnp.full((1, tq), _NEG_INF, F32)
    mx = run(lambda j, c: score_chunk(j, c, None), qi, (neg, neg))
    for dgl in range(per_q):
        mx = score_chunk(qi * per_q + dgl, mx, dgl * tk)

    def value_chunk(j, carry):
        l0, l1, acc = carry
        p0 = jnp.exp(s_ref[j, 0:tk, :] - mx[0])
        p1 = jnp.exp(s_ref[j, tk:2 * tk, :] - mx[1])
        vt = vt_ref[0, j]
        zv = jnp.zeros_like(vt)
        vbd = jnp.concatenate([jnp.where(row_v < hd, vt, zv), jnp.where(row_v >= hd, vt, zv)], axis=1)
        pt = jnp.concatenate([p0.astype(BF16), p1.astype(BF16)], axis=0)
        acc = acc + jnp.dot(vbd, pt, preferred_element_type=F32)
        return (l0 + jnp.sum(p0, axis=0, keepdims=True), l1 + jnp.sum(p1, axis=0, keepdims=True), acc)

    zero = jnp.zeros((1, tq), F32)
    l0, l1, acc = run(value_chunk, qi + 1, (zero, zero, jnp.zeros((LANES, tq), F32)))
    o_ref[...] = (acc * (1.0 / jnp.where(row_a < hd, l0, l1))).T


def _fox_out_kernel(x_ref, o_ref_in, og_ref, gt_ref, w_ref, out_ref):
    y = (o_ref_in[...] * og_ref[...]).astype(BF16)
    out_ref[...] = x_ref[...] + gt_ref[0] * jnp.dot(y, w_ref[...], preferred_element_type=F32)


def _fox_layer(x2, n_batch, seq, g, shift, scale, gate, in_w, f_b, qg, kg, out_w):
    n, d = x2.shape
    nh = FOX_HEADS
    tm = 256
    tk = FOX_TK
    nst = seq // tm
    w_main = jnp.concatenate([in_w[:, :3 * d], in_w[:, 3 * d + nh:]], axis=1).astype(BF16)
    wf = jnp.pad(in_w[:, 3 * d:3 * d + nh], ((0, 0), (0, LANES - nh))).astype(BF16)
    fb = jnp.pad(f_b, (0, LANES - nh)).reshape(1, LANES)
    qg_t = jnp.tile(qg, nh).reshape(1, d)
    kg_t = jnp.tile(kg, nh).reshape(1, d)
    ii = jnp.arange(LANES) // FOX_HEAD_DIM
    bd = jnp.where(ii[:, None] == ii[None, :], 1.0 / FOX_HEAD_DIM, 0.0).astype(BF16)
    const2 = lambda b, s: (0, 0)
    vec = pl.BlockSpec((1, 1, d), lambda b, s: (b, 0, 0))
    tok = pl.BlockSpec((tm, d), lambda b, s: (b * nst + s, 0))
    q, k, v, og, cumt, cum = pl.pallas_call(
        functools.partial(_fox_proj_kernel, tm=tm, d=d),
        grid=(n_batch, nst),
        in_specs=[tok, pl.BlockSpec((1, d), const2), vec, vec,
                  pl.BlockSpec((d, 4 * d), const2),
                  pl.BlockSpec((d, LANES), const2),
                  pl.BlockSpec((1, LANES), const2),
                  pl.BlockSpec((1, d), const2),
                  pl.BlockSpec((1, d), const2),
                  pl.BlockSpec((LANES, LANES), const2)],
        out_specs=[pl.BlockSpec((1, 1, d, tm), lambda b, s: (b, s, 0, 0)), tok,
                   pl.BlockSpec((1, tm // tk, d, tk), lambda b, s: (b, s, 0, 0)), tok,
                   pl.BlockSpec((1, nh, tm), lambda b, s: (b, 0, s)),
                   pl.BlockSpec((tm, LANES), lambda b, s: (b * nst + s, 0))],
        out_shape=[jax.ShapeDtypeStruct((n_batch, nst, d, tm), BF16), jax.ShapeDtypeStruct((n, d), BF16),
                   jax.ShapeDtypeStruct((n_batch, seq // tk, d, tk), BF16), jax.ShapeDtypeStruct((n, d), F32),
                   jax.ShapeDtypeStruct((n_batch, nh, seq), F32),
                   jax.ShapeDtypeStruct((n, LANES), F32)],
        scratch_shapes=[pltpu.VMEM((1, LANES), F32)],
        compiler_params=_cparams(("arbitrary", "arbitrary")),
        name="fox_proj",
    )(x2, g.reshape(1, d), shift, scale, w_main, wf, fb, qg_t, kg_t, bd)

    tq = tm
    nq = seq // tq
    cumt4 = cumt.reshape(n_batch, nh, nq, tq)
    o = pl.pallas_call(
        functools.partial(_fox_attn_kernel, tq=tq, tk=tk, seq=seq),
        grid=(n_batch, nh // 2, nq),
        in_specs=[pl.BlockSpec((1, 1, LANES, tq), lambda b, hp, qi: (b, qi, hp, 0)),
                  pl.BlockSpec((seq, LANES), lambda b, hp, qi: (b, hp)),
                  pl.BlockSpec((1, seq // tk, LANES, tk), lambda b, hp, qi: (b, 0, hp, 0)),
                  pl.BlockSpec((seq, LANES), lambda b, hp, qi: (b, 0)),
                  pl.BlockSpec((1, nh, nq, tq), lambda b, hp, qi: (b, 0, 0, 0))],
        out_specs=pl.BlockSpec((tq, LANES), lambda b, hp, qi: (b * nq + qi, hp)),
        out_shape=jax.ShapeDtypeStruct((n, d), F32),
        scratch_shapes=[pltpu.VMEM((2, seq, LANES), F32),
                        pltpu.VMEM((seq // tk, 2 * tk, tq), F32)],
        compiler_params=_cparams(("arbitrary", "arbitrary", "arbitrary")),
        name="fox_attn",
    )(q, k, v, cum, cumt4)

    to = 512
    tok2 = pl.BlockSpec((to, d), lambda i: (i, 0))
    return pl.pallas_call(
        _fox_out_kernel,
        grid=(n // to,),
        in_specs=[tok2, tok2, tok2,
                  pl.BlockSpec((1, 1, d), lambda i: (i // (seq // to), 0, 0)),
                  pl.BlockSpec((d, d), lambda i: (0, 0))],
        out_specs=tok2,
        out_shape=jax.ShapeDtypeStruct((n, d), F32),
        compiler_params=_cparams(("arbitrary",)),
        name="fox_out",
    )(x2, o, og, gate, out_w.astype(BF16))


def kernel(x, c, l0_mix_norm_g, l0_mix_mod_w, l0_mix_mod_b, l0_lru_in_w, l0_lru_conv_w, l0_lru_conv_b, l0_lru_ra_w, l0_lru_ra_b, l0_lru_ri_w, l0_lru_ri_b, l0_lru_lambda, l0_lru_out_w, l0_ffn_norm_g, l0_ffn_mod_w, l0_ffn_mod_b, l0_peer_q_w, l0_peer_subkey1, l0_peer_subkey2, l0_peer_u, l0_peer_v, l1_mix_norm_g, l1_mix_mod_w, l1_mix_mod_b, l1_fox_in_w, l1_fox_f_b, l1_fox_q_norm_g, l1_fox_k_norm_g, l1_fox_out_w, l1_ffn_norm_g, l1_ffn_mod_w, l1_ffn_mod_b, l1_peer_q_w, l1_peer_subkey1, l1_peer_subkey2, l1_peer_u, l1_peer_v):
    n_batch, seq, d = x.shape
    x2 = x.reshape(n_batch * seq, d)
    c_pad = jnp.pad(c, ((0, SUBLANES - n_batch % SUBLANES), (0, 0))) if n_batch % SUBLANES else c

    sh, sc, gt = _ada_mod(c_pad, l0_mix_mod_w, l0_mix_mod_b, n_batch)
    x2 = _lru_layer(x2, n_batch, seq, l0_mix_norm_g, sh, sc, gt, l0_lru_in_w, l0_lru_conv_w, l0_lru_conv_b,
                    l0_lru_ra_w, l0_lru_ra_b, l0_lru_ri_w, l0_lru_ri_b, l0_lru_lambda, l0_lru_out_w)
    sh, sc, gt = _ada_mod(c_pad, l0_ffn_mod_w, l0_ffn_mod_b, n_batch)
    x2 = _peer_layer(x2, n_batch, seq, l0_ffn_norm_g, sh, sc, gt, l0_peer_q_w, l0_peer_subkey1, l0_peer_subkey2,
                     l0_peer_u, l0_peer_v)
    sh, sc, gt = _ada_mod(c_pad, l1_mix_mod_w, l1_mix_mod_b, n_batch)
    x2 = _fox_layer(x2, n_batch, seq, l1_mix_norm_g, sh, sc, gt, l1_fox_in_w, l1_fox_f_b, l1_fox_q_norm_g,
                    l1_fox_k_norm_g, l1_fox_out_w)
    sh, sc, gt = _ada_mod(c_pad, l1_ffn_mod_w, l1_ffn_mod_b, n_batch)
    x2 = _peer_layer(x2, n_batch, seq, l1_ffn_norm_g, sh, sc, gt, l1_peer_q_w, l1_peer_subkey1, l1_peer_subkey2,
                     l1_peer_u, l1_peer_v)
    return x2.reshape(n_batch, seq, d)
```
